```python
import math
import jax
import jax.numpy as jnp
from jax import lax
import numpy as np

D_MODEL = 1024
BATCH = 2
SEQ = 8192
DEPTH = 4
DEC_BATCH = 32
DEC_SEQ = 1
PAST_LEN = 8192
PAGE_SIZE = 128

N_MIXERS = 4
N_MLSTM = (DEPTH + 3) // 4
N_DSA = (DEPTH + 2) // 4
N_RGLRU = (DEPTH + 1) // 4
N_FOX = DEPTH // 4
RMS_EPS = 1e-6
ATTN_BLOCK = 128
NEG_INF = float('-inf')
D_FF = 4 * D_MODEL

MLSTM_HEADS = 8
MLSTM_DQK = D_MODEL // 16
MLSTM_DV = D_MODEL // 8
MLSTM_CHUNK = 64
MLSTM_IN = 2 * MLSTM_HEADS * MLSTM_DQK + 2 * MLSTM_HEADS * MLSTM_DV + 2 * MLSTM_HEADS
MLSTM_SPLITS = (MLSTM_HEADS * MLSTM_DQK, 2 * MLSTM_HEADS * MLSTM_DQK,
                2 * MLSTM_HEADS * MLSTM_DQK + MLSTM_HEADS * MLSTM_DV,
                2 * MLSTM_HEADS * MLSTM_DQK + 2 * MLSTM_HEADS * MLSTM_DV)

DSA_HEADS = 16
DSA_DH = D_MODEL // DSA_HEADS
IDX_HEADS = 8
IDX_DIM = 64
DSA_TOPK = 256
DSA_HD = DSA_HEADS * DSA_DH
DSA_IN = 3 * DSA_HD + IDX_HEADS * IDX_DIM + IDX_DIM + IDX_HEADS
DSA_SPLITS = (DSA_HD, 2 * DSA_HD, 3 * DSA_HD, 3 * DSA_HD + IDX_HEADS * IDX_DIM,
              3 * DSA_HD + IDX_HEADS * IDX_DIM + IDX_DIM)
IDX_SCALE = (IDX_DIM * IDX_HEADS) ** -0.5

T5_BUCKETS = 32
T5_MAX_EXACT = 16
T5_MAX_DISTANCE = 128

RG_WIDTH = 5 * D_MODEL // 4
RG_BLOCKS = 10
RG_BLOCK = RG_WIDTH // RG_BLOCKS
RG_CONV = 4
RG_C = 8.0

FOX_HEADS = 16
FOX_DH = D_MODEL // FOX_HEADS
FOX_HD = FOX_HEADS * FOX_DH
FOX_IN = 3 * FOX_HD + FOX_HEADS

kernel_name = 'hybrid_mlstm_dsa_rglru_fox_step'


def rmsnorm(x, g):
    x32 = x.astype(jnp.float32)
    y = x32 * lax.rsqrt(jnp.mean(x32 * x32, axis=-1, keepdims=True) + RMS_EPS) * g.astype(jnp.float32)
    return y.astype(x.dtype)


def sq_relu_mlp(x, w1, w2):
    hdn = jax.nn.relu(jnp.einsum('bld,df->blf', x, w1))
    return jnp.einsum('blf,fd->bld', hdn * hdn, w2)


def mlstm_chunk(state, inp):
    c0, n0, m0 = state
    q, k, v, ig, lf = inp
    t_len = q.shape[2]
    b = jnp.cumsum(lf, axis=-1)
    causal = jnp.tril(jnp.ones((t_len, t_len), dtype=bool))
    log_w = jnp.where(causal, b[..., :, None] - b[..., None, :] + ig[..., None, :], NEG_INF)
    log_inter = b + m0[..., None]
    m = jnp.maximum(log_inter, jnp.max(log_w, axis=-1))
    w_intra = jnp.exp(log_w - m[..., None])
    w_inter = jnp.exp(log_inter - m)
    s = jnp.einsum('bhtd,bhsd->bhts', q, k) * w_intra
    num = w_inter[..., None] * jnp.einsum('bhvd,bhtd->bhtv', c0, q) + jnp.einsum('bhts,bhsv->bhtv', s, v)
    den = w_inter * jnp.einsum('bhd,bhtd->bht', n0, q) + jnp.sum(s, axis=-1)
    h = num / jnp.maximum(jnp.abs(den), jnp.exp(-m))[..., None]
    m_last = m[..., -1]
    g = jnp.exp(b[..., -1:] - b + ig - m_last[..., None])
    decay = jnp.exp(b[..., -1] + m0 - m_last)
    c_new = decay[..., None, None] * c0 + jnp.einsum('bhs,bhsv,bhsd->bhvd', g, v, k)
    n_new = decay[..., None] * n0 + jnp.einsum('bhs,bhsd->bhd', g, k)
    return (c_new, n_new, m_last), h


def mlstm_mixer(x, w_in, b_gate, norm_g, w_out, c0, n0, m0, chunk_len):
    bsz, seq_len, _ = x.shape
    f32 = jnp.float32
    proj = jnp.einsum('bld,de->ble', x, w_in).astype(f32)
    q, k, v, o, gates = jnp.split(proj, MLSTM_SPLITS, axis=-1)
    gates = gates + b_gate.astype(f32)
    ig = jnp.swapaxes(gates[..., :MLSTM_HEADS], 1, 2)
    lf = jnp.swapaxes(jax.nn.log_sigmoid(gates[..., MLSTM_HEADS:]), 1, 2)

    def heads(t, dh):
        return t.reshape(bsz, seq_len, MLSTM_HEADS, dh).transpose(0, 2, 1, 3)

    q = heads(q, MLSTM_DQK) * MLSTM_DQK ** -0.5
    k = heads(k, MLSTM_DQK)
    v = heads(v, MLSTM_DV)
    n_chunks = seq_len // chunk_len

    def to_chunks(t):
        t = t.reshape(t.shape[:2] + (n_chunks, chunk_len) + t.shape[3:])
        return jnp.moveaxis(t, 2, 0)

    state, hc = lax.scan(mlstm_chunk, (c0.astype(f32), n0.astype(f32), m0.astype(f32)),
                         tuple(to_chunks(t) for t in (q, k, v, ig, lf)))
    hh = jnp.moveaxis(hc, 0, 2).reshape(bsz, MLSTM_HEADS, seq_len, MLSTM_DV)
    hh = hh * lax.rsqrt(jnp.mean(hh * hh, axis=-1, keepdims=True) + RMS_EPS) * norm_g.astype(f32)[None, :, None, :]
    hh = hh.transpose(0, 2, 1, 3).reshape(bsz, seq_len, MLSTM_HEADS * MLSTM_DV) * jax.nn.sigmoid(o)
    y = jnp.einsum('ble,ed->bld', hh.astype(x.dtype), w_out)
    c_new, n_new, m_new = state
    return y, (c_new.astype(x.dtype), n_new.astype(x.dtype), m_new.astype(x.dtype))


def t5_bucket(dist):
    n = jnp.maximum(dist, 0)
    nf = jnp.maximum(n, 1).astype(jnp.float32)
    large = T5_MAX_EXACT + (jnp.log(nf / T5_MAX_EXACT) / math.log(T5_MAX_DISTANCE / T5_MAX_EXACT)
                            * (T5_BUCKETS - T5_MAX_EXACT)).astype(jnp.int32)
    large = jnp.minimum(large, T5_BUCKETS - 1)
    return jnp.where(n < T5_MAX_EXACT, n, large)


def dsa_project(x, w_in):
    bsz, seq_len, _ = x.shape
    q, k, v, qi, ki, wi = jnp.split(jnp.einsum('bld,de->ble', x, w_in), DSA_SPLITS, axis=-1)
    shp = (bsz, seq_len, DSA_HEADS, DSA_DH)
    return (q.reshape(shp), k.reshape(shp), v.reshape(shp),
            qi.reshape(bsz, seq_len, IDX_HEADS, IDX_DIM), ki, wi)


def index_scores(qi, wi, ki):
    dots = jax.nn.relu(jnp.einsum('bqhd,bsd->bqhs', qi.astype(jnp.float32), ki.astype(jnp.float32)))
    return jnp.einsum('bqh,bqhs->bqs', wi.astype(jnp.float32) * IDX_SCALE, dots)


def gather_rows(rows, idx):
    return jax.vmap(lambda r, i: r[i])(rows, idx)


def sparse_attend(q, k_sel, v_sel, dist, valid, rel_bias):
    s = jnp.einsum('bqhd,bqkhd->bhqk', q, k_sel).astype(jnp.float32) * DSA_DH ** -0.5
    s = s + rel_bias.astype(jnp.float32)[t5_bucket(dist)].transpose(0, 3, 1, 2)
    s = jnp.where(valid[:, None], s, NEG_INF)
    p = jax.nn.softmax(s, axis=-1)
    return jnp.einsum('bhqk,bqkhd->bqhd', p.astype(v_sel.dtype), v_sel)


def dsa_prompt(x, w_in, w_out, rel_bias):
    q, k, v, qi, ki, wi = dsa_project(x, w_in)
    bsz, seq_len = x.shape[:2]
    top = min(DSA_TOPK, seq_len // 4)
    kpos = jnp.arange(seq_len)

    def block(i):
        start = i * ATTN_BLOCK
        sl = lambda t: lax.dynamic_slice_in_dim(t, start, ATTN_BLOCK, axis=1)
        qpos = start + jnp.arange(ATTN_BLOCK)
        score = index_scores(sl(qi), sl(wi), ki)
        score = jnp.where(kpos[None, None, :] <= qpos[None, :, None], score, NEG_INF)
        val, idx = lax.top_k(score, top)
        return sparse_attend(sl(q), gather_rows(k, idx), gather_rows(v, idx),
                             qpos[None, :, None] - idx, jnp.isfinite(val), rel_bias)

    o = lax.map(block, jnp.arange(seq_len // ATTN_BLOCK))
    o = jnp.moveaxis(o, 0, 1).reshape(bsz, seq_len, DSA_HD)
    return jnp.einsum('ble,ed->bld', o, w_out), k, v, ki


def dsa_sample(x, w_in, w_out, rel_bias, pool_k, pool_v, pool_ki, page_table):
    q, k, v, qi, ki, wi = dsa_project(x, w_in)
    bsz, n_new = x.shape[:2]
    past = page_table.shape[1] * PAGE_SIZE
    total = past + n_new
    top = min(DSA_TOPK, total // 4)
    past_ki = pool_ki[page_table].reshape(bsz, past, IDX_DIM)
    all_ki = jnp.concatenate([past_ki.astype(ki.dtype), ki], axis=1)
    qpos = past + jnp.arange(n_new)
    kpos = jnp.arange(total)
    score = index_scores(qi, wi, all_ki)
    score = jnp.where(kpos[None, None, :] <= qpos[None, :, None], score, NEG_INF)
    val, idx = lax.top_k(score, top)
    idx_past = jnp.minimum(idx, past - 1)
    phys = jax.vmap(lambda pt, ix: pt[ix // PAGE_SIZE])(page_table, idx_past)
    off = idx_past % PAGE_SIZE
    from_new = (idx >= past)[..., None, None]
    idx_new = jnp.clip(idx - past, 0, n_new - 1)
    k_sel = jnp.where(from_new, gather_rows(k, idx_new), pool_k[phys, off].astype(k.dtype))
    v_sel = jnp.where(from_new, gather_rows(v, idx_new), pool_v[phys, off].astype(v.dtype))
    o = sparse_attend(q, k_sel, v_sel, qpos[None, :, None] - idx, jnp.isfinite(val), rel_bias)
    o = o.reshape(bsz, n_new, DSA_HD)
    return jnp.einsum('ble,ed->bld', o, w_out), k, v, ki


def lin_combine(left, right):
    a_l, b_l = left
    a_r, b_r = right
    return a_l * a_r, a_r * b_l + b_r


def rglru_mixer(x, w_in, conv_w, conv_b, w_a, b_a, w_x, b_x, lam, w_out, h0, conv0):
    bsz, seq_len, _ = x.shape
    f32 = jnp.float32
    gate, xb = jnp.split(jnp.einsum('bld,de->ble', x, w_in), 2, axis=-1)
    xpad = jnp.concatenate([conv0.astype(xb.dtype), xb], axis=1)
    conv = conv_b + sum(xpad[:, j:j + seq_len] * conv_w[j] for j in range(RG_CONV))
    new_conv = xpad[:, seq_len:]
    c32 = conv.astype(f32)
    xr = c32.reshape(bsz, seq_len, RG_BLOCKS, RG_BLOCK)
    r = jax.nn.sigmoid(jnp.einsum('blnc,ncd->blnd', xr, w_a.astype(f32)).reshape(bsz, seq_len, RG_WIDTH) + b_a)
    i = jax.nn.sigmoid(jnp.einsum('blnc,ncd->blnd', xr, w_x.astype(f32)).reshape(bsz, seq_len, RG_WIDTH) + b_x)
    log_a = -RG_C * r * jax.nn.softplus(-lam.astype(f32))
    a = jnp.exp(log_a)
    u = jnp.sqrt(-jnp.expm1(2.0 * log_a)) * (i * c32)
    u = u.at[:, 0].add(a[:, 0] * h0.astype(f32))
    _, hseq = lax.associative_scan(lin_combine, (a, u), axis=1)
    y = (hseq * jax.nn.gelu(gate.astype(f32))).astype(x.dtype)
    return jnp.einsum('ble,ed->bld', y, w_out), (hseq[:, -1].astype(x.dtype), new_conv.astype(x.dtype))


def fox_project(x, w_in, b_f):
    bsz, seq_len, _ = x.shape
    q, k, v, f = jnp.split(jnp.einsum('bld,de->ble', x, w_in), (FOX_HD, 2 * FOX_HD, 3 * FOX_HD), axis=-1)
    shp = (bsz, seq_len, FOX_HEADS, FOX_DH)
    logf = jax.nn.log_sigmoid(f.astype(jnp.float32) + b_f.astype(jnp.float32))
    return q.reshape(shp), k.reshape(shp), v.reshape(shp), logf


def fox_attend(q, c_q, qpos, k, v, c, kpos):
    s = jnp.einsum('bqhd,bkhd->bhqk', q, k).astype(jnp.float32) * FOX_DH ** -0.5
    s = s + jnp.swapaxes(c_q, 1, 2)[..., :, None] - jnp.swapaxes(c, 1, 2)[..., None, :]
    s = jnp.where(kpos[None, None, None, :] <= qpos[None, None, :, None], s, NEG_INF)
    p = jax.nn.softmax(s, axis=-1)
    return jnp.einsum('bhqk,bkhd->bqhd', p.astype(v.dtype), v)


def fox_prompt(x, w_in, b_f, w_out):
    q, k, v, logf = fox_project(x, w_in, b_f)
    bsz, seq_len = x.shape[:2]
    c = jnp.cumsum(logf, axis=1)
    kpos = jnp.arange(seq_len)

    def block(i):
        start = i * ATTN_BLOCK
        sl = lambda t: lax.dynamic_slice_in_dim(t, start, ATTN_BLOCK, axis=1)
        return fox_attend(sl(q), sl(c), start + jnp.arange(ATTN_BLOCK), k, v, c, kpos)

    o = lax.map(block, jnp.arange(seq_len // ATTN_BLOCK))
    o = jnp.moveaxis(o, 0, 1).reshape(bsz, seq_len, FOX_HD)
    return jnp.einsum('ble,ed->bld', o, w_out), k, v, logf.astype(x.dtype)


def fox_sample(x, w_in, b_f, w_out, pool_k, pool_v, pool_logf, page_table):
    q, k, v, logf = fox_project(x, w_in, b_f)
    bsz, n_new = x.shape[:2]
    past = page_table.shape[1] * PAGE_SIZE
    all_k = jnp.concatenate([pool_k[page_table].reshape(bsz, past, FOX_HEADS, FOX_DH).astype(k.dtype), k], axis=1)
    all_v = jnp.concatenate([pool_v[page_table].reshape(bsz, past, FOX_HEADS, FOX_DH).astype(v.dtype), v], axis=1)
    all_logf = jnp.concatenate([pool_logf[page_table].reshape(bsz, past, FOX_HEADS).astype(jnp.float32), logf], axis=1)
    c = jnp.cumsum(all_logf, axis=1)
    o = fox_attend(q, c[:, past:], past + jnp.arange(n_new), all_k, all_v, c, jnp.arange(past + n_new))
    o = o.reshape(bsz, n_new, FOX_HD)
    return jnp.einsum('ble,ed->bld', o, w_out), k, v, logf.astype(x.dtype)


def setup_inputs(seed: int = 0) -> dict:
    key = jax.random.key(seed)
    keys = iter(jax.random.split(key, 48))

    def nrm(shape, scale=1.0):
        return jax.random.normal(next(keys), shape, jnp.float32) * scale

    def unif(shape, lo, hi):
        return jax.random.uniform(next(keys), shape, jnp.float32, lo, hi)

    d = D_MODEL
    n_pages = PAST_LEN // PAGE_SIZE
    n_pool = (5 * DEC_BATCH * n_pages + 3) // 4
    page_table = jax.random.permutation(next(keys), n_pool)[: DEC_BATCH * n_pages]
    page_table = page_table.reshape(DEC_BATCH, n_pages).astype(jnp.int32)
    return {
        'x_prompt': nrm((BATCH, SEQ, d)),
        'x_sample': nrm((DEC_BATCH, DEC_SEQ, d)),
        'state_mlstm_C': nrm((N_MLSTM, DEC_BATCH, MLSTM_HEADS, MLSTM_DV, MLSTM_DQK), 0.5),
        'state_mlstm_n': nrm((N_MLSTM, DEC_BATCH, MLSTM_HEADS, MLSTM_DQK)),
        'state_mlstm_m': nrm((N_MLSTM, DEC_BATCH, MLSTM_HEADS)),
        'cache_dsa_k': nrm((N_DSA, n_pool, PAGE_SIZE, DSA_HEADS, DSA_DH)),
        'cache_dsa_v': nrm((N_DSA, n_pool, PAGE_SIZE, DSA_HEADS, DSA_DH)),
        'cache_dsa_kidx': nrm((N_DSA, n_pool, PAGE_SIZE, IDX_DIM)),
        'state_rglru_h': nrm((N_RGLRU, DEC_BATCH, RG_WIDTH), 0.5),
        'state_rglru_conv': nrm((N_RGLRU, DEC_BATCH, RG_CONV - 1, RG_WIDTH)),
        'cache_fox_k': nrm((N_FOX, n_pool, PAGE_SIZE, FOX_HEADS, FOX_DH)),
        'cache_fox_v': nrm((N_FOX, n_pool, PAGE_SIZE, FOX_HEADS, FOX_DH)),
        'cache_fox_logf': jax.nn.log_sigmoid(nrm((N_FOX, n_pool, PAGE_SIZE, FOX_HEADS)) + 3.0),
        'page_table': page_table,
        'norm_mix': 1.0 + nrm((DEPTH, d), 0.05),
        'norm_mlp': 1.0 + nrm((DEPTH, d), 0.05),
        'norm_final': 1.0 + nrm((d,), 0.05),
        'mlstm_w_in': nrm((N_MLSTM, d, MLSTM_IN), d ** -0.5),
        'mlstm_b_gate': jnp.concatenate([nrm((N_MLSTM, MLSTM_HEADS), 0.5),
                                         unif((N_MLSTM, MLSTM_HEADS), 3.0, 6.0)], axis=-1),
        'mlstm_norm': 1.0 + nrm((N_MLSTM, MLSTM_HEADS, MLSTM_DV), 0.05),
        'mlstm_w_out': nrm((N_MLSTM, MLSTM_HEADS * MLSTM_DV, d), (MLSTM_HEADS * MLSTM_DV) ** -0.5),
        'dsa_w_in': nrm((N_DSA, d, DSA_IN), d ** -0.5),
        'dsa_w_out': nrm((N_DSA, DSA_HD, d), DSA_HD ** -0.5),
        'rel_bias': nrm((T5_BUCKETS, DSA_HEADS), 0.5),
        'rglru_w_in': nrm((N_RGLRU, d, 2 * RG_WIDTH), d ** -0.5),
        'rglru_conv_w': nrm((N_RGLRU, RG_CONV, RG_WIDTH), 0.5),
        'rglru_conv_b': nrm((N_RGLRU, RG_WIDTH), 0.02),
        'rglru_w_a': nrm((N_RGLRU, RG_BLOCKS, RG_BLOCK, RG_BLOCK), RG_BLOCK ** -0.5),
        'rglru_b_a': nrm((N_RGLRU, RG_WIDTH), 0.1),
        'rglru_w_x': nrm((N_RGLRU, RG_BLOCKS, RG_BLOCK, RG_BLOCK), RG_BLOCK ** -0.5),
        'rglru_b_x': nrm((N_RGLRU, RG_WIDTH), 0.1),
        'rglru_lam': unif((N_RGLRU, RG_WIDTH), 4.3, 9.0),
        'rglru_w_out': nrm((N_RGLRU, RG_WIDTH, d), RG_WIDTH ** -0.5),
        'fox_w_in': nrm((N_FOX, d, FOX_IN), d ** -0.5),
        'fox_b_f': unif((N_FOX, FOX_HEADS), 1.0, 6.0),
        'fox_w_out': nrm((N_FOX, FOX_HD, d), FOX_HD ** -0.5),
        'mlp_w1': nrm((DEPTH, d, D_FF), d ** -0.5),
        'mlp_w2': nrm((DEPTH, D_FF, d), D_FF ** -0.5),
    }


def reference(x_prompt, x_sample, state_mlstm_C, state_mlstm_n, state_mlstm_m,
              cache_dsa_k, cache_dsa_v, cache_dsa_kidx, state_rglru_h, state_rglru_conv,
              cache_fox_k, cache_fox_v, cache_fox_logf, page_table,
              norm_mix, norm_mlp, norm_final,
              mlstm_w_in, mlstm_b_gate, mlstm_norm, mlstm_w_out,
              dsa_w_in, dsa_w_out, rel_bias,
              rglru_w_in, rglru_conv_w, rglru_conv_b, rglru_w_a, rglru_b_a, rglru_w_x, rglru_b_x,
              rglru_lam, rglru_w_out,
              fox_w_in, fox_b_f, fox_w_out, mlp_w1, mlp_w2):
    yp, ys = x_prompt, x_sample
    bsz = x_prompt.shape[0]
    f32 = jnp.float32
    m_p, m_s, d_p, d_s, r_p, r_s, f_p, f_s = ([] for _ in range(8))
    for layer in range(DEPTH):
        kind, inst = layer % N_MIXERS, layer // N_MIXERS
        hp = rmsnorm(yp, norm_mix[layer])
        hs = rmsnorm(ys, norm_mix[layer])
        if kind == 0:
            c0 = jnp.zeros((bsz, MLSTM_HEADS, MLSTM_DV, MLSTM_DQK), f32)
            n0 = jnp.zeros((bsz, MLSTM_HEADS, MLSTM_DQK), f32)
            m0 = jnp.full((bsz, MLSTM_HEADS), NEG_INF, f32)
            op, st_p = mlstm_mixer(hp, mlstm_w_in[inst], mlstm_b_gate[inst], mlstm_norm[inst], mlstm_w_out[inst],
                                   c0, n0, m0, MLSTM_CHUNK)
            os_, st_s = mlstm_mixer(hs, mlstm_w_in[inst], mlstm_b_gate[inst], mlstm_norm[inst], mlstm_w_out[inst],
                                    state_mlstm_C[inst], state_mlstm_n[inst], state_mlstm_m[inst], hs.shape[1])
            m_p.append(st_p)
            m_s.append(st_s)
        elif kind == 1:
            op, kp, vp, kip = dsa_prompt(hp, dsa_w_in[inst], dsa_w_out[inst], rel_bias)
            os_, ks_, vs_, kis = dsa_sample(hs, dsa_w_in[inst], dsa_w_out[inst], rel_bias,
                                            cache_dsa_k[inst], cache_dsa_v[inst], cache_dsa_kidx[inst], page_table)
            d_p.append((kp, vp, kip))
            d_s.append((ks_, vs_, kis))
        elif kind == 2:
            h0 = jnp.zeros((bsz, RG_WIDTH), f32)
            conv0 = jnp.zeros((bsz, RG_CONV - 1, RG_WIDTH), x_prompt.dtype)
            op, st_p = rglru_mixer(hp, rglru_w_in[inst], rglru_conv_w[inst], rglru_conv_b[inst], rglru_w_a[inst],
                                   rglru_b_a[inst], rglru_w_x[inst], rglru_b_x[inst], rglru_lam[inst],
                                   rglru_w_out[inst], h0, conv0)
            os_, st_s = rglru_mixer(hs, rglru_w_in[inst], rglru_conv_w[inst], rglru_conv_b[inst], rglru_w_a[inst],
                                    rglru_b_a[inst], rglru_w_x[inst], rglru_b_x[inst], rglru_lam[inst],
                                    rglru_w_out[inst], state_rglru_h[inst], state_rglru_conv[inst])
            r_p.append(st_p)
            r_s.append(st_s)
        else:
            op, kp, vp, lfp = fox_prompt(hp, fox_w_in[inst], fox_b_f[inst], fox_w_out[inst])
            os_, ks_, vs_, lfs = fox_sample(hs, fox_w_in[inst], fox_b_f[inst], fox_w_out[inst],
                                            cache_fox_k[inst], cache_fox_v[inst], cache_fox_logf[inst], page_table)
            f_p.append((kp, vp, lfp))
            f_s.append((ks_, vs_, lfs))
        yp = yp + op
        ys = ys + os_
        yp = yp + sq_relu_mlp(rmsnorm(yp, norm_mlp[layer]), mlp_w1[layer], mlp_w2[layer])
        ys = ys + sq_relu_mlp(rmsnorm(ys, norm_mlp[layer]), mlp_w1[layer], mlp_w2[layer])
    y_prompt = rmsnorm(yp, norm_final)
    y_sample = rmsnorm(ys, norm_final)

    def stk(lst, j):
        return jnp.stack([entry[j] for entry in lst])

    return (y_prompt, y_sample,
            stk(m_p, 0), stk(m_p, 1), stk(m_p, 2), stk(m_s, 0), stk(m_s, 1), stk(m_s, 2),
            stk(d_p, 0), stk(d_p, 1), stk(d_p, 2), stk(d_s, 0), stk(d_s, 1), stk(d_s, 2),
            stk(r_p, 0), stk(r_p, 1), stk(r_s, 0), stk(r_s, 1),
            stk(f_p, 0), stk(f_p, 1), stk(f_p, 2), stk(f_s, 0), stk(f_s, 1), stk(f_s, 2))
```

```python
import functools
import math

import numpy as np
import jax
import jax.numpy as jnp
from jax import lax
from jax.experimental import pallas as pl
from jax.experimental.pallas import tpu as pltpu

F32 = jnp.float32
BF16 = jnp.bfloat16
I32 = jnp.int32
NEG_INF = float("-inf")

RMS_EPS = 1e-6
D_MODEL = 1024
PAGE_SIZE = 128

MLSTM_HEADS = 8
MLSTM_DQK = 64
MLSTM_DV = 128
MLSTM_CHUNK = 128

DSA_HEADS = 16
DSA_DH = 64
IDX_HEADS = 8
IDX_DIM = 64
DSA_TOPK = 256
IDX_SCALE = (IDX_DIM * IDX_HEADS) ** -0.5
T5_BUCKETS = 32
T5_MAX_EXACT = 16
T5_MAX_DISTANCE = 128

RG_WIDTH = 1280
RG_BLOCKS = 10
RG_BLOCK = 128
RG_CONV = 4
RG_C = 8.0

FOX_HEADS = 16
FOX_DH = 64

VMEM_BUDGET_BYTES = 56 * 1024 * 1024
KEY_NEG_INF = np.int32(-2139095041)
INT_MIN = np.int32(-2147483648)
INT_MAX = np.int32(2147483647)


def _cparams(n_grid, vmem_mb=32):
    return pltpu.CompilerParams(dimension_semantics=("arbitrary",) * n_grid,
                                vmem_limit_bytes=min(vmem_mb * 1024 * 1024, VMEM_BUDGET_BYTES))


def _log_sigmoid(x):
    return jnp.minimum(x, 0.0) - jnp.log1p(jnp.exp(-jnp.abs(x)))


def _rms(x, g):
    return x * lax.rsqrt(jnp.mean(x * x, axis=-1, keepdims=True) + RMS_EPS) * g


def _dot(a, b):
    return jnp.dot(a, b, preferred_element_type=F32)


def _dot_t(a, b):
    return lax.dot_general(a, b, (((1,), (1,)), ((), ())), preferred_element_type=F32)


def _norm_proj_kernel(x_ref, g_ref, *refs, plan):
    n_w = len(plan)
    w_refs = refs[:n_w]
    o_refs = refs[n_w:]
    xn = _rms(x_ref[...], g_ref[...]).astype(BF16)
    oi = 0
    for w_ref, (transposed, dtypes) in zip(w_refs, plan):
        o = _dot_t(w_ref[...], xn) if transposed else _dot(xn, w_ref[...])
        for dt in dtypes:
            o_refs[oi][...] = o.astype(dt)
            oi += 1


def norm_proj(x, g, segments, tm):
    n_rows, d = x.shape
    assert n_rows % tm == 0
    plan, w_args, w_specs, out_shapes, out_specs = [], [], [], [], []
    for w, transposed, dtypes in segments:
        n = w.shape[1]
        wb = w.astype(BF16)
        if transposed:
            wb = wb.T
            w_specs.append(pl.BlockSpec((n, d), lambda i: (0, 0)))
        else:
            w_specs.append(pl.BlockSpec((d, n), lambda i: (0, 0)))
        w_args.append(wb)
        plan.append((transposed, tuple(dtypes)))
        for dt in dtypes:
            if transposed:
                out_shapes.append(jax.ShapeDtypeStruct((n, n_rows), dt))
                out_specs.append(pl.BlockSpec((n, tm), lambda i: (0, i)))
            else:
                out_shapes.append(jax.ShapeDtypeStruct((n_rows, n), dt))
                out_specs.append(pl.BlockSpec((tm, n), lambda i: (i, 0)))
    return pl.pallas_call(
        functools.partial(_norm_proj_kernel, plan=tuple(plan)),
        grid=(n_rows // tm,),
        in_specs=[pl.BlockSpec((tm, d), lambda i: (i, 0)), pl.BlockSpec((1, d), lambda i: (0, 0))] + w_specs,
        out_specs=out_specs,
        out_shape=out_shapes,
        compiler_params=_cparams(1, 48),
        name="norm_proj",
    )(x, g.reshape(1, d), *w_args)


def _out_proj_kernel(a_ref, w_ref, r_ref, o_ref):
    o_ref[...] = r_ref[...] + _dot(a_ref[...], w_ref[...])


def out_proj(a, w, resid, tm):
    n_rows, k = a.shape
    d = w.shape[1]
    return pl.pallas_call(
        _out_proj_kernel,
        grid=(n_rows // tm,),
        in_specs=[pl.BlockSpec((tm, k), lambda i: (i, 0)), pl.BlockSpec((k, d), lambda i: (0, 0)),
                  pl.BlockSpec((tm, d), lambda i: (i, 0))],
        out_specs=pl.BlockSpec((tm, d), lambda i: (i, 0)),
        out_shape=jax.ShapeDtypeStruct((n_rows, d), F32),
        compiler_params=_cparams(1, 32),
        name="out_proj",
    )(a, w.astype(BF16), resid)


def _mlp_kernel(x_ref, g_ref, w1_ref, w2_ref, gf_ref, o_ref, xn_ref, acc_ref, *, final_norm):
    f = pl.program_id(1)

    @pl.when(f == 0)
    def _():
        xn_ref[...] = _rms(x_ref[...], g_ref[...]).astype(BF16)
        acc_ref[...] = jnp.zeros_like(acc_ref)

    h = jnp.maximum(_dot(xn_ref[...], w1_ref[...]), 0.0)
    acc_ref[...] += _dot((h * h).astype(BF16), w2_ref[...])

    @pl.when(f == pl.num_programs(1) - 1)
    def _():
        y = x_ref[...] + acc_ref[...]
        o_ref[...] = _rms(y, gf_ref[...]) if final_norm else y


def mlp(x, g, w1, w2, g_final, final_norm, tm, tf):
    n_rows, d = x.shape
    d_ff = w1.shape[1]
    return pl.pallas_call(
        functools.partial(_mlp_kernel, final_norm=final_norm),
        grid=(n_rows // tm, d_ff // tf),
        in_specs=[pl.BlockSpec((tm, d), lambda i, f: (i, 0)), pl.BlockSpec((1, d), lambda i, f: (0, 0)),
                  pl.BlockSpec((d, tf), lambda i, f: (0, f)), pl.BlockSpec((tf, d), lambda i, f: (f, 0)),
                  pl.BlockSpec((1, d), lambda i, f: (0, 0))],
        out_specs=pl.BlockSpec((tm, d), lambda i, f: (i, 0)),
        out_shape=jax.ShapeDtypeStruct((n_rows, d), F32),
        scratch_shapes=[pltpu.VMEM((tm, d), BF16), pltpu.VMEM((tm, d), F32)],
        compiler_params=_cparams(2, 48),
        name="mlp",
    )(x, g.reshape(1, d), w1.astype(BF16), w2.astype(BF16), g_final.reshape(1, d))


def _lane_cumsum(x):
    lane = lax.broadcasted_iota(I32, x.shape, 1)
    d = 1
    while d < x.shape[1]:
        x = x + jnp.where(lane >= d, pltpu.roll(x, d, axis=1), 0.0)
        d *= 2
    return x


def _lane_cummax(x):
    lane = lax.broadcasted_iota(I32, x.shape, 1)
    d = 1
    while d < x.shape[1]:
        x = jnp.maximum(x, jnp.where(lane >= d, pltpu.roll(x, d, axis=1), NEG_INF))
        d *= 2
    return x


def _mlstm_prompt_kernel(q_ref, kt_ref, v_ref, o_ref, gt_ref, bg_ref, ng_ref,
                         hh_ref, cst_ref, mst_ref, ct_ref, m_ref):
    c = pl.program_id(1)
    L = q_ref.shape[0]
    H, DQK, DV = MLSTM_HEADS, MLSTM_DQK, MLSTM_DV

    @pl.when(c == 0)
    def _():
        ct_ref[...] = jnp.zeros_like(ct_ref)
        m_ref[...] = jnp.full_like(m_ref, NEG_INF)

    gates = gt_ref[...] + bg_ref[...]
    ig = gates[:H]
    lf = _log_sigmoid(gates[H:])
    b = _lane_cumsum(lf)
    a = ig - b
    m0 = m_ref[...]
    big_m = jnp.maximum(m0, _lane_cummax(a))
    m = b + big_m
    cols = jnp.concatenate([big_m, m, jnp.zeros((128 - 2 * H, L), F32)], axis=0).T
    row_i = lax.broadcasted_iota(I32, (L, L), 0)
    col_i = lax.broadcasted_iota(I32, (L, L), 1)
    causal = col_i <= row_i
    ones_col = (lax.broadcasted_iota(I32, (L, DV), 1) == 0).astype(BF16)
    for h in range(H):
        q = q_ref[:, h * DQK:(h + 1) * DQK]
        kt = kt_ref[h * DQK:(h + 1) * DQK, :]
        v_ext = jnp.concatenate([v_ref[:, h * DV:(h + 1) * DV], ones_col], axis=1)
        big_m_col = cols[:, h:h + 1]
        m_col = cols[:, H + h:H + h + 1]
        w = jnp.exp(jnp.where(causal, a[h:h + 1, :] - big_m_col, NEG_INF))
        w_inter = jnp.exp(m0[h:h + 1, :] - big_m_col)
        s = _dot(q, kt) * w
        ct = ct_ref[h]
        tot = w_inter * _dot(q, ct.astype(BF16)) + _dot(s.astype(BF16), v_ext)
        num = tot[:, :DV]
        den = tot[:, DV:DV + 1]
        hv = num / jnp.maximum(jnp.abs(den), jnp.exp(-m_col))
        hv = _rms(hv, ng_ref[h:h + 1, :])
        gate = jax.nn.sigmoid(o_ref[:, h * DV:(h + 1) * DV])
        hh_ref[:, h * DV:(h + 1) * DV] = (hv * gate).astype(hh_ref.dtype)
        g_row = w[L - 1:L, :]
        decay = w_inter[L - 1:L, :]
        ct_ref[h] = decay * ct + _dot((kt.astype(F32) * g_row).astype(BF16), v_ext)
    m_ref[...] = m[:, L - 1:L]

    @pl.when(c == pl.num_programs(1) - 1)
    def _():
        cst_ref[0] = ct_ref[...]
        mst_ref[0] = m_ref[...]


def mlstm_prompt(q, kt, v, o, gt, b_gate, norm_g, bsz, seq):
    H, DQK, DV, L = MLSTM_HEADS, MLSTM_DQK, MLSTM_DV, MLSTM_CHUNK
    nc = seq // L
    n_rows = bsz * seq
    row = lambda b_, c_: (b_ * nc + c_, 0)
    col = lambda b_, c_: (0, b_ * nc + c_)
    fixed = lambda b_, c_: (0, 0)
    hh, cst, mst = pl.pallas_call(
        _mlstm_prompt_kernel,
        grid=(bsz, nc),
        in_specs=[pl.BlockSpec((L, H * DQK), row), pl.BlockSpec((H * DQK, L), col),
                  pl.BlockSpec((L, H * DV), row), pl.BlockSpec((L, H * DV), row),
                  pl.BlockSpec((2 * H, L), col), pl.BlockSpec((2 * H, 1), fixed),
                  pl.BlockSpec((H, DV), fixed)],
        out_specs=[pl.BlockSpec((L, H * DV), row),
                   pl.BlockSpec((1, H, DQK, 2 * DV), lambda b_, c_: (b_, 0, 0, 0)),
                   pl.BlockSpec((1, H, 1), lambda b_, c_: (b_, 0, 0))],
        out_shape=[jax.ShapeDtypeStruct((n_rows, H * DV), BF16),
                   jax.ShapeDtypeStruct((bsz, H, DQK, 2 * DV), F32),
                   jax.ShapeDtypeStruct((bsz, H, 1), F32)],
        scratch_shapes=[pltpu.VMEM((H, DQK, 2 * DV), F32), pltpu.VMEM((H, 1), F32)],
        compiler_params=_cparams(2, 32),
        name="mlstm_prompt",
    )(q, kt, v, o, gt, b_gate.reshape(2 * H, 1), norm_g)
    c_new = jnp.swapaxes(cst[..., :DV], 2, 3)
    n_new = cst[..., DV]
    return hh, c_new, n_new, mst[..., 0]


def _mlstm_sample_kernel(q_ref, k_ref, v_ref, vt_ref, o_ref, g_ref, bg_ref, ng_ref, c0_ref, n0_ref, m0_ref,
                         hh_ref, c_ref, n_ref, m_ref):
    H = MLSTM_HEADS
    gates = g_ref[0] + bg_ref[...]
    ig = gates[:H]
    lf = _log_sigmoid(gates[H:])
    m0 = m0_ref[0]
    m = jnp.maximum(lf + m0, ig)
    w_inter = jnp.exp(lf + m0 - m)
    w_intra = jnp.exp(ig - m)
    qb = q_ref[0]
    q32 = qb.astype(F32)
    k = k_ref[0]
    v = v_ref[0]
    s = jnp.sum(q32 * k.astype(BF16).astype(F32), axis=-1, keepdims=True) * w_intra
    n0 = n0_ref[0]
    den = w_inter * jnp.sum(q32 * n0.astype(BF16).astype(F32), axis=-1, keepdims=True) + s
    row_i = lax.broadcasted_iota(I32, (H, MLSTM_DV), 0)
    inter = jnp.zeros((H, MLSTM_DV), F32)
    for h in range(H):
        c0 = c0_ref[0, h]
        inter = jnp.where(row_i == h, _dot_t(qb, c0.astype(BF16)), inter)
        c_ref[0, h] = w_inter[h:h + 1, :] * c0 + (w_intra[h:h + 1, :] * vt_ref[0, :, h:h + 1]) * k[h:h + 1, :]
    num = w_inter * inter + s * v.astype(BF16).astype(F32)
    hv = num / jnp.maximum(jnp.abs(den), jnp.exp(-m))
    hv = _rms(hv, ng_ref[...])
    hh_ref[0] = (hv * jax.nn.sigmoid(o_ref[0])).astype(hh_ref.dtype)
    n_ref[0] = w_inter * n0 + w_intra * k
    m_ref[0] = m


def mlstm_sample(q, k, v, o, gates, b_gate, norm_g, c0, n0, m0):
    H, DQK, DV = MLSTM_HEADS, MLSTM_DQK, MLSTM_DV
    bsz = q.shape[0]
    q3 = q.reshape(bsz, H, DQK)
    k3 = k.reshape(bsz, H, DQK)
    v3 = v.reshape(bsz, H, DV)
    vt = jnp.swapaxes(v3, 1, 2)
    b3 = lambda *blk: pl.BlockSpec((1,) + blk, lambda i: (i,) + (0,) * len(blk))
    fixed = lambda *blk: pl.BlockSpec(blk, lambda i: (0,) * len(blk))
    hh, c_new, n_new, m_new = pl.pallas_call(
        _mlstm_sample_kernel,
        grid=(bsz,),
        in_specs=[b3(H, DQK), b3(H, DQK), b3(H, DV), b3(DV, H), b3(H, DV), b3(2 * H, 1), fixed(2 * H, 1),
                  fixed(H, DV), b3(H, DV, DQK), b3(H, DQK), b3(H, 1)],
        out_specs=[b3(H, DV), b3(H, DV, DQK), b3(H, DQK), b3(H, 1)],
        out_shape=[jax.ShapeDtypeStruct((bsz, H, DV), BF16), jax.ShapeDtypeStruct((bsz, H, DV, DQK), F32),
                   jax.ShapeDtypeStruct((bsz, H, DQK), F32), jax.ShapeDtypeStruct((bsz, H, 1), F32)],
        compiler_params=_cparams(1, 16),
        name="mlstm_sample",
    )(q3, k3, v3, vt, o.reshape(bsz, H, DV), gates.reshape(bsz, 2 * H, 1), b_gate.reshape(2 * H, 1), norm_g,
      c0, n0, m0.reshape(bsz, H, 1))
    return hh.reshape(bsz, H * DV), c_new, n_new, m_new[..., 0]


def mlstm_layer(xp, xs, g, w_in, b_gate, norm_g, w_out, c0, n0, m0, bsz, seq):
    H, DQK, DV = MLSTM_HEADS, MLSTM_DQK, MLSTM_DV
    e_q, e_k, e_v, e_o = H * DQK, 2 * H * DQK, 2 * H * DQK + H * DV, 2 * H * DQK + 2 * H * DV
    wq = w_in[:, :e_q] * (DQK ** -0.5)
    wk, wv, wo, wg = w_in[:, e_q:e_k], w_in[:, e_k:e_v], w_in[:, e_v:e_o], w_in[:, e_o:]
    q, kt, v, o, gt = norm_proj(xp, g, [(wq, False, [BF16]), (wk, True, [BF16]), (wv, False, [BF16]),
                                        (wo, False, [F32]), (wg, True, [F32])], tm=512)
    hh, c_p, n_p, m_p = mlstm_prompt(q, kt, v, o, gt, b_gate, norm_g, bsz, seq)
    yp = out_proj(hh, w_out, xp, tm=512)
    qs, ks, vs, os_, gs = norm_proj(xs, g, [(wq, False, [BF16]), (wk, False, [F32]), (wv, False, [F32]),
                                            (wo, False, [F32]), (wg, False, [F32])], tm=xs.shape[0])
    hhs, c_s, n_s, m_s = mlstm_sample(qs, ks, vs, os_, gs, b_gate, norm_g, c0, n0, m0)
    ys = out_proj(hhs, w_out, xs, tm=xs.shape[0])
    return yp, ys, (c_p, n_p, m_p), (c_s, n_s, m_s)


def _softplus(x):
    return jnp.maximum(x, 0.0) + jnp.log1p(jnp.exp(-jnp.abs(x)))


def _expm1(x):
    u = jnp.exp(x)
    um1 = u - 1.0
    return jnp.where(um1 == 0.0, x, jnp.where(um1 == -1.0, -1.0, um1 * x / jnp.log(u)))


def _rglru_gates(conv, n, wa_ref, wx_ref, ba_ref, bx_ref, lam_ref):
    sl = slice(n * RG_BLOCK, (n + 1) * RG_BLOCK)
    cb = conv.astype(BF16)
    r = jax.nn.sigmoid(_dot(cb, wa_ref[n]) + ba_ref[:, sl])
    i = jax.nn.sigmoid(_dot(cb, wx_ref[n]) + bx_ref[:, sl])
    log_a = -RG_C * r * _softplus(-lam_ref[:, sl])
    a = jnp.exp(log_a)
    u = jnp.sqrt(-_expm1(2.0 * log_a)) * (i * conv)
    return a, u


def _rglru_prompt_kernel(gate_ref, xb_ref, cw_ref, cb_ref, wa_ref, wx_ref, ba_ref, bx_ref, lam_ref,
                         y_ref, h_ref, tail_ref, xpad_ref, hc_ref):
    c = pl.program_id(1)
    T = xb_ref.shape[0]

    @pl.when(c == 0)
    def _():
        xpad_ref[0:8, :] = jnp.zeros((8, RG_WIDTH), F32)
        hc_ref[...] = jnp.zeros_like(hc_ref)

    xpad_ref[8:8 + T, :] = xb_ref[...]
    row = lax.broadcasted_iota(I32, (T, RG_BLOCK), 0)
    for n in range(RG_BLOCKS):
        sl = slice(n * RG_BLOCK, (n + 1) * RG_BLOCK)
        conv = cb_ref[:, sl]
        for j in range(RG_CONV):
            conv = conv + cw_ref[j:j + 1, sl] * xpad_ref[8 - (RG_CONV - 1) + j:8 - (RG_CONV - 1) + j + T, sl]
        a, u = _rglru_gates(conv, n, wa_ref, wx_ref, ba_ref, bx_ref, lam_ref)
        d = 1
        while d < T:
            keep = row >= d
            a_sh = jnp.where(keep, pltpu.roll(a, d, axis=0), 1.0)
            u_sh = jnp.where(keep, pltpu.roll(u, d, axis=0), 0.0)
            u = a * u_sh + u
            a = a * a_sh
            d *= 2
        h = a * hc_ref[:, sl] + u
        hc_ref[:, sl] = h[T - 1:T, :]
        y_ref[:, sl] = (h * jax.nn.gelu(gate_ref[:, sl])).astype(y_ref.dtype)
    xpad_ref[0:8, :] = xb_ref[T - 8:T, :]

    @pl.when(c == pl.num_programs(1) - 1)
    def _():
        h_ref[0] = hc_ref[...]
        tail_ref[0] = xb_ref[T - 8:T, :]


def rglru_prompt(gate, xb, conv_w, conv_b, w_a, b_a, w_x, b_x, lam, bsz, seq, tt):
    W = RG_WIDTH
    nc = seq // tt
    row = lambda b_, c_: (b_ * nc + c_, 0)
    fixed2 = lambda b_, c_: (0, 0)
    fixed3 = lambda b_, c_: (0, 0, 0)
    y, h, tail = pl.pallas_call(
        _rglru_prompt_kernel,
        grid=(bsz, nc),
        in_specs=[pl.BlockSpec((tt, W), row), pl.BlockSpec((tt, W), row),
                  pl.BlockSpec((RG_CONV, W), fixed2), pl.BlockSpec((1, W), fixed2),
                  pl.BlockSpec((RG_BLOCKS, RG_BLOCK, RG_BLOCK), fixed3),
                  pl.BlockSpec((RG_BLOCKS, RG_BLOCK, RG_BLOCK), fixed3),
                  pl.BlockSpec((1, W), fixed2), pl.BlockSpec((1, W), fixed2), pl.BlockSpec((1, W), fixed2)],
        out_specs=[pl.BlockSpec((tt, W), row), pl.BlockSpec((1, 1, W), lambda b_, c_: (b_, 0, 0)),
                   pl.BlockSpec((1, 8, W), lambda b_, c_: (b_, 0, 0))],
        out_shape=[jax.ShapeDtypeStruct((bsz * seq, W), BF16), jax.ShapeDtypeStruct((bsz, 1, W), F32),
                   jax.ShapeDtypeStruct((bsz, 8, W), F32)],
        scratch_shapes=[pltpu.VMEM((tt + 8, W), F32), pltpu.VMEM((1, W), F32)],
        compiler_params=_cparams(2, 32),
        name="rglru_prompt",
    )(gate, xb, conv_w, conv_b.reshape(1, W), w_a.astype(BF16), w_x.astype(BF16),
      b_a.reshape(1, W), b_x.reshape(1, W), lam.reshape(1, W))
    return y, h[:, 0], tail[:, 8 - (RG_CONV - 1):]


def _rglru_sample_kernel(gate_ref, xb_ref, c0_ref, c1_ref, c2_ref, h0_ref, cw_ref, cb_ref, wa_ref, wx_ref,
                         ba_ref, bx_ref, lam_ref, y_ref, h_ref):
    taps = (c0_ref, c1_ref, c2_ref, xb_ref)
    for n in range(RG_BLOCKS):
        sl = slice(n * RG_BLOCK, (n + 1) * RG_BLOCK)
        conv = cb_ref[:, sl]
        for j in range(RG_CONV):
            conv = conv + cw_ref[j:j + 1, sl] * taps[j][:, sl]
        a, u = _rglru_gates(conv, n, wa_ref, wx_ref, ba_ref, bx_ref, lam_ref)
        h = a * h0_ref[:, sl] + u
        h_ref[:, sl] = h
        y_ref[:, sl] = (h * jax.nn.gelu(gate_ref[:, sl])).astype(y_ref.dtype)


def rglru_sample(gate, xb, conv0, h0, conv_w, conv_b, w_a, b_a, w_x, b_x, lam):
    W = RG_WIDTH
    bsz = gate.shape[0]
    full2 = lambda r: pl.BlockSpec((r, W), lambda i: (0, 0))
    full3 = pl.BlockSpec((RG_BLOCKS, RG_BLOCK, RG_BLOCK), lambda i: (0, 0, 0))
    y, h = pl.pallas_call(
        _rglru_sample_kernel,
        grid=(1,),
        in_specs=[full2(bsz)] * 6 + [full2(RG_CONV), full2(1), full3, full3, full2(1), full2(1), full2(1)],
        out_specs=[full2(bsz), full2(bsz)],
        out_shape=[jax.ShapeDtypeStruct((bsz, W), BF16), jax.ShapeDtypeStruct((bsz, W), F32)],
        compiler_params=_cparams(1, 16),
        name="rglru_sample",
    )(gate, xb, conv0[:, 0], conv0[:, 1], conv0[:, 2], h0, conv_w, conv_b.reshape(1, W),
      w_a.astype(BF16), w_x.astype(BF16), b_a.reshape(1, W), b_x.reshape(1, W), lam.reshape(1, W))
    new_conv = jnp.concatenate([conv0[:, 1:], xb[:, None, :]], axis=1)
    return y, h, new_conv


def rglru_layer(xp, xs, g, w_in, conv_w, conv_b, w_a, b_a, w_x, b_x, lam, w_out, h0, conv0, bsz, seq):
    W = RG_WIDTH
    segs = [(w_in[:, :W], False, [F32]), (w_in[:, W:], False, [F32])]
    gate, xb = norm_proj(xp, g, segs, tm=512)
    y, h_p, conv_p = rglru_prompt(gate, xb, conv_w, conv_b, w_a, b_a, w_x, b_x, lam, bsz, seq, tt=256)
    yp = out_proj(y, w_out, xp, tm=512)
    gate_s, xb_s = norm_proj(xs, g, segs, tm=xs.shape[0])
    y_s, h_s, conv_s = rglru_sample(gate_s, xb_s, conv0, h0, conv_w, conv_b, w_a, b_a, w_x, b_x, lam)
    ys = out_proj(y_s, w_out, xs, tm=xs.shape[0])
    return yp, ys, (h_p, conv_p), (h_s, conv_s)


ATT_HEADS = 16
ATT_DH = 64
ATT_PAIRS = ATT_HEADS // 2
LANES = 128


def _split_heads(q_ref, qh_ref):
    tq = q_ref.shape[0]
    low_half = lax.broadcasted_iota(I32, (tq, LANES), 1) < ATT_DH
    for j in range(ATT_PAIRS):
        qp = q_ref[:, j * LANES:(j + 1) * LANES]
        zero = jnp.zeros_like(qp)
        qh_ref[2 * j] = jnp.where(low_half, qp, zero)
        qh_ref[2 * j + 1] = jnp.where(low_half, zero, qp)


def _flash_update(h, s, v_pair, m_ref, l_ref, acc_ref):
    m_old = m_ref[h]
    m_new = jnp.maximum(m_old, jnp.max(s, axis=-1, keepdims=True))
    m_safe = jnp.where(m_new == NEG_INF, 0.0, m_new)
    alpha = jnp.exp(m_old - m_safe)
    p = jnp.exp(s - m_safe)
    l_ref[h] = alpha * l_ref[h] + jnp.sum(p, axis=-1, keepdims=True)
    acc_ref[h] = alpha * acc_ref[h] + _dot(p.astype(BF16), v_pair)
    m_ref[h] = m_new


def _flash_finish(o_ref, l_ref, acc_ref):
    tq = o_ref.shape[0]
    low_half = lax.broadcasted_iota(I32, (tq, LANES), 1) < ATT_DH
    for j in range(ATT_PAIRS):
        lo = acc_ref[2 * j] / l_ref[2 * j]
        hi = acc_ref[2 * j + 1] / l_ref[2 * j + 1]
        o_ref[:, j * LANES:(j + 1) * LANES] = jnp.where(low_half, lo, hi).astype(o_ref.dtype)


def _flash_init(m_ref, l_ref, acc_ref):
    m_ref[...] = jnp.full_like(m_ref, NEG_INF)
    l_ref[...] = jnp.zeros_like(l_ref)
    acc_ref[...] = jnp.zeros_like(acc_ref)


def _fox_gate_kernel(ft_ref, f_ref, bft_ref, bf_ref, ct_ref, logf_ref, carry_ref):
    c = pl.program_id(1)
    ts = ft_ref.shape[1]

    @pl.when(c == 0)
    def _():
        carry_ref[...] = jnp.zeros_like(carry_ref)

    cs = _lane_cumsum(_log_sigmoid(ft_ref[...] + bft_ref[...])) + carry_ref[...]
    ct_ref[...] = cs
    carry_ref[...] = cs[:, ts - 1:ts]
    logf_ref[...] = _log_sigmoid(f_ref[...] + bf_ref[...])


def fox_gates(ft, f, b_f, bsz, seq, ts):
    H = FOX_HEADS
    nc = seq // ts
    return pl.pallas_call(
        _fox_gate_kernel,
        grid=(bsz, nc),
        in_specs=[pl.BlockSpec((H, ts), lambda b_, c_: (0, b_ * nc + c_)),
                  pl.BlockSpec((ts, H), lambda b_, c_: (b_ * nc + c_, 0)),
                  pl.BlockSpec((H, 1), lambda b_, c_: (0, 0)), pl.BlockSpec((1, H), lambda b_, c_: (0, 0))],
        out_specs=[pl.BlockSpec((H, ts), lambda b_, c_: (0, b_ * nc + c_)),
                   pl.BlockSpec((ts, H), lambda b_, c_: (b_ * nc + c_, 0))],
        out_shape=[jax.ShapeDtypeStruct(ft.shape, F32), jax.ShapeDtypeStruct(f.shape, F32)],
        scratch_shapes=[pltpu.VMEM((H, 1), F32)],
        compiler_params=_cparams(2, 16),
        name="fox_gates",
    )(ft, f, b_f.reshape(H, 1), b_f.reshape(1, H))


def _fox_prompt_kernel(q_ref, kt_ref, v_ref, ct_ref, o_ref, qh_ref, m_ref, l_ref, acc_ref):
    qi = pl.program_id(1)
    tq = q_ref.shape[0]
    qs = pl.multiple_of(qi * tq, tq)
    _split_heads(q_ref, qh_ref)
    _flash_init(m_ref, l_ref, acc_ref)
    c_q0 = ct_ref[:, pl.ds(qs, LANES)][:, 0:1]
    causal = (lax.broadcasted_iota(I32, (tq, tq), 1) <= lax.broadcasted_iota(I32, (tq, tq), 0))

    def block(kb, diagonal):
        ks = pl.multiple_of(kb * tq, tq)
        decay = c_q0 - ct_ref[:, pl.ds(ks, tq)]
        for h in range(ATT_HEADS):
            j = h // 2
            s = _dot(qh_ref[h], kt_ref[j * LANES:(j + 1) * LANES, pl.ds(ks, tq)]) + decay[h:h + 1, :]
            if diagonal:
                s = jnp.where(causal, s, NEG_INF)
            _flash_update(h, s, v_ref[pl.ds(ks, tq), j * LANES:(j + 1) * LANES], m_ref, l_ref, acc_ref)

    def far(kb, carry):
        block(kb, False)
        return carry

    lax.fori_loop(0, qi, far, 0)
    block(qi, True)
    _flash_finish(o_ref, l_ref, acc_ref)


def fox_prompt_attend(q, kt, v, ct, bsz, seq, tq):
    hd = ATT_HEADS * ATT_DH
    nq = seq // tq
    once = pl.Buffered(1)
    return pl.pallas_call(
        _fox_prompt_kernel,
        grid=(bsz, nq),
        in_specs=[pl.BlockSpec((tq, hd), lambda b_, i: (b_ * nq + i, 0)),
                  pl.BlockSpec((hd, seq), lambda b_, i: (0, b_), pipeline_mode=once),
                  pl.BlockSpec((seq, hd), lambda b_, i: (b_, 0), pipeline_mode=once),
                  pl.BlockSpec((ATT_HEADS, seq), lambda b_, i: (0, b_), pipeline_mode=once)],
        out_specs=pl.BlockSpec((tq, hd), lambda b_, i: (b_ * nq + i, 0)),
        out_shape=jax.ShapeDtypeStruct((bsz * seq, hd), BF16),
        scratch_shapes=[pltpu.VMEM((ATT_HEADS, tq, LANES), BF16), pltpu.VMEM((ATT_HEADS, tq, 1), F32),
                        pltpu.VMEM((ATT_HEADS, tq, 1), F32), pltpu.VMEM((ATT_HEADS, tq, LANES), F32)],
        compiler_params=_cparams(2, 56),
        name="fox_prompt",
    )(q, kt, v, ct)


PAGES_PER_STEP = 8


def _paged_attend_kernel(pt_ref, q_ref, kn_ref, vn_ref, *refs, mode):
    pp = PAGES_PER_STEP
    k_refs, v_refs = refs[:pp], refs[pp:2 * pp]
    refs = refs[2 * pp:]
    if mode == "fox":
        lf_refs, lfn_ref = refs[:pp], refs[pp]
        o_ref, m_ref, l_ref, acc_ref, cc_ref = refs[pp + 1:]
    else:
        mask_ref, maskn_ref, bias_ref, biasn_ref = refs[:4]
        o_ref, m_ref, l_ref, acc_ref = refs[4:]
    j = pl.program_id(1)
    hd = ATT_HEADS * ATT_DH

    @pl.when(j == 0)
    def _():
        _flash_init(m_ref, l_ref, acc_ref)
        if mode == "fox":
            cc_ref[...] = jnp.zeros_like(cc_ref)

    sel = (lax.broadcasted_iota(I32, (ATT_HEADS, hd), 1) // ATT_DH) == lax.broadcasted_iota(I32, (ATT_HEADS, hd), 0)
    q_rows = jnp.broadcast_to(q_ref[0].astype(F32), (ATT_HEADS, hd))
    qexp = jnp.where(sel, q_rows, 0.0).astype(BF16)

    def update(s, pv_fn):
        m_old = m_ref[...]
        m_new = jnp.maximum(m_old, jnp.max(s, axis=-1, keepdims=True))
        m_safe = jnp.where(m_new == NEG_INF, 0.0, m_new)
        alpha = jnp.exp(m_old - m_safe)
        p = jnp.exp(s - m_safe)
        l_ref[...] = alpha * l_ref[...] + jnp.sum(p, axis=-1, keepdims=True)
        acc_ref[...] = alpha * acc_ref[...] + pv_fn(p)
        m_ref[...] = m_new

    for p_i in range(pp):
        s = _dot_t(qexp, k_refs[p_i][0].astype(BF16))
        if mode == "fox":
            c = _lane_cumsum(lf_refs[p_i][0]) + cc_ref[...]
            cc_ref[...] = c[:, PAGE_SIZE - 1:PAGE_SIZE]
            s = s - c
        else:
            sl = slice(p_i * PAGE_SIZE, (p_i + 1) * PAGE_SIZE)
            s = s + mask_ref[0][:, sl] + bias_ref[:, sl]
        vb = v_refs[p_i][0].astype(BF16)
        update(s, lambda p, vb=vb: _dot(p.astype(BF16), vb))

    @pl.when(j == pl.num_programs(1) - 1)
    def _():
        kn = kn_ref[0].astype(BF16).astype(F32)
        s = jnp.sum(qexp.astype(F32) * kn, axis=-1, keepdims=True)
        if mode == "fox":
            s = s - (cc_ref[...] + lfn_ref[0])
        else:
            s = s + maskn_ref[0][:, 0:1] + biasn_ref[:, 0:1]
        vn = vn_ref[0].astype(BF16).astype(F32)
        update(s, lambda p: p.astype(BF16).astype(F32) * vn)
        out = acc_ref[...] / l_ref[...]
        o_ref[0] = jnp.sum(jnp.where(sel, out, 0.0), axis=0, keepdims=True).astype(o_ref.dtype)


def paged_attend(mode, page_table, q, k_new, v_new, pool_k, pool_v, extras):
    pp = PAGES_PER_STEP
    bsz, hd = q.shape
    n_pool = pool_k.shape[0]
    n_pages = page_table.shape[1]
    past = n_pages * PAGE_SIZE
    assert n_pages % pp == 0
    row3 = pl.BlockSpec((1, 1, hd), lambda b_, j, pt: (b_, 0, 0))
    page = lambda p_i, *blk: pl.BlockSpec((1,) + blk, lambda b_, j, pt, p_i=p_i: (pt[b_, j * pp + p_i],) + (0,) * len(blk))
    in_specs = [row3, row3, row3]
    in_specs += [page(p_i, PAGE_SIZE, hd) for p_i in range(pp)] * 2
    args = [q.reshape(bsz, 1, hd), k_new.reshape(bsz, 1, hd), v_new.reshape(bsz, 1, hd)]
    args += [pool_k.reshape(n_pool, PAGE_SIZE, hd)] * pp + [pool_v.reshape(n_pool, PAGE_SIZE, hd)] * pp
    scratch = [pltpu.VMEM((ATT_HEADS, 1), F32), pltpu.VMEM((ATT_HEADS, 1), F32), pltpu.VMEM((ATT_HEADS, hd), F32)]
    if mode == "fox":
        pool_lft, lf_new = extras
        in_specs += [page(p_i, ATT_HEADS, PAGE_SIZE) for p_i in range(pp)]
        in_specs += [pl.BlockSpec((1, ATT_HEADS, 1), lambda b_, j, pt: (b_, 0, 0))]
        args += [pool_lft] * pp + [lf_new]
        scratch += [pltpu.VMEM((ATT_HEADS, 1), F32)]
    else:
        mask, bias = extras
        in_specs += [pl.BlockSpec((1, 1, pp * PAGE_SIZE), lambda b_, j, pt: (b_, 0, j)),
                     pl.BlockSpec((1, 1, LANES), lambda b_, j, pt: (b_, 0, past // LANES)),
                     pl.BlockSpec((ATT_HEADS, pp * PAGE_SIZE), lambda b_, j, pt: (0, j)),
                     pl.BlockSpec((ATT_HEADS, LANES), lambda b_, j, pt: (0, past // LANES))]
        args += [mask, mask, bias, bias]
    out = pl.pallas_call(
        functools.partial(_paged_attend_kernel, mode=mode),
        grid_spec=pltpu.PrefetchScalarGridSpec(
            num_scalar_prefetch=1, grid=(bsz, n_pages // pp), in_specs=in_specs,
            out_specs=pl.BlockSpec((1, 1, hd), lambda b_, j, pt: (b_, 0, 0)), scratch_shapes=scratch),
        out_shape=jax.ShapeDtypeStruct((bsz, 1, hd), BF16),
        compiler_params=_cparams(2, 48),
        name="paged_attend_" + mode,
    )(page_table, *args)
    return out.reshape(bsz, hd)


def fox_layer(xp, xs, g, w_in, b_f, w_out, pool_k, pool_v, pool_logf, page_table, bsz, seq):
    hd = FOX_HEADS * FOX_DH
    wq = w_in[:, :hd] * (FOX_DH ** -0.5)
    wk, wv, wf = w_in[:, hd:2 * hd], w_in[:, 2 * hd:3 * hd], w_in[:, 3 * hd:]
    q, k, kt, v, vb, f, ft = norm_proj(xp, g, [(wq, False, [BF16]), (wk, False, [F32]), (wk, True, [BF16]),
                                               (wv, False, [F32, BF16]), (wf, False, [F32]), (wf, True, [F32])],
                                       tm=512)
    ct, logf = fox_gates(ft, f, b_f, bsz, seq, ts=min(seq, 2048))
    o = fox_prompt_attend(q, kt, vb, ct, bsz, seq, tq=256)
    yp = out_proj(o, w_out, xp, tm=512)
    dbs = xs.shape[0]
    qs, ks, vs, fs = norm_proj(xs, g, [(wq, False, [BF16]), (wk, False, [F32]), (wv, False, [F32]),
                                       (wf, False, [F32])], tm=dbs)
    lf_s = fox_logf_rows(fs, b_f)
    os_ = paged_attend("fox", page_table, qs, ks, vs, pool_k, pool_v,
                       (jnp.swapaxes(pool_logf, 1, 2), lf_s.reshape(dbs, FOX_HEADS, 1)))
    ys = out_proj(os_, w_out, xs, tm=dbs)
    shp = lambda t, n: t.reshape(n, -1, FOX_HEADS, FOX_DH)
    return (yp, ys, (shp(k, bsz), shp(v, bsz), logf.reshape(bsz, seq, FOX_HEADS)),
            (shp(ks, dbs), shp(vs, dbs), lf_s.reshape(dbs, 1, FOX_HEADS)))


def _ordered_key(x):
    b = lax.bitcast_convert_type(x + 0.0, I32)
    return b ^ ((b >> 31) & INT_MAX)


def _as_f32(x):
    return lax.bitcast_convert_type(x, F32)


def _as_i32(x):
    return lax.bitcast_convert_type(x, I32)


def _topk_to_mask(sc_ref, n_blocks, rows, topk):
    lane = lax.broadcasted_iota(I32, (rows, LANES), 1)
    k_f = float(topk)

    def count(pred):
        def body(kb, acc):
            ks = pl.multiple_of(kb * LANES, LANES)
            return acc + jnp.where(pred(_as_i32(sc_ref[:, pl.ds(ks, LANES)]), ks), 1.0, 0.0)
        acc = lax.fori_loop(0, n_blocks, body, jnp.zeros((rows, LANES), F32))
        return jnp.sum(acc, axis=-1, keepdims=True)

    t0 = jnp.where(count(lambda key, ks: key >= 0) >= k_f, 0, INT_MIN).astype(I32)

    def bit_step(i, t):
        cand = t | (1 << (30 - i))
        return jnp.where(count(lambda key, ks: key >= cand) >= k_f, cand, t)

    t = lax.fori_loop(0, 31, bit_step, t0)
    need = k_f - count(lambda key, ks: key > t)
    n_eq = count(lambda key, ks: key == t)

    def tie_search(_):
        def idx_step(i, x):
            cand = x | (1 << (30 - i))
            below = count(lambda key, ks: (key == t) & (lane + ks < cand))
            return jnp.where(below < need, cand, x)
        return lax.fori_loop(0, 31, idx_step, jnp.zeros((rows, 1), I32))

    x = lax.cond(jnp.max(n_eq - need) > 0.0, tie_search, lambda _: jnp.full((rows, 1), INT_MAX, I32), 0)

    def write(kb, carry):
        ks = pl.multiple_of(kb * LANES, LANES)
        key = _as_i32(sc_ref[:, pl.ds(ks, LANES)])
        sel = ((key > t) | ((key == t) & (lane + ks <= x))) & (key > KEY_NEG_INF)
        sc_ref[:, pl.ds(ks, LANES)] = jnp.where(sel, 0.0, NEG_INF)
        return carry

    lax.fori_loop(0, n_blocks, write, 0)


def _dsa_prompt_kernel(q_ref, qi_ref, wi_ref, kit_ref, kt_ref, v_ref, bias_ref, o_ref,
                       sc_ref, qh_ref, qih_ref, m_ref, l_ref, acc_ref, *, topk):
    blk = pl.program_id(1)
    tq = q_ref.shape[0]
    qs = blk * tq
    row = lax.broadcasted_iota(I32, (tq, LANES), 0)
    lane = lax.broadcasted_iota(I32, (tq, LANES), 1)

    low_half = lane < IDX_DIM
    for j in range(IDX_HEADS // 2):
        qp = qi_ref[:, j * LANES:(j + 1) * LANES]
        zero = jnp.zeros_like(qp)
        qih_ref[2 * j] = jnp.where(low_half, qp, zero)
        qih_ref[2 * j + 1] = jnp.where(low_half, zero, qp)
    wi = wi_ref[...] * IDX_SCALE

    def score_block(kb, carry):
        ks = pl.multiple_of(kb * LANES, LANES)
        kit = kit_ref[:, pl.ds(ks, LANES)]
        sc = jnp.zeros((tq, LANES), F32)
        for h in range(IDX_HEADS):
            sc = sc + wi[:, h:h + 1] * jnp.maximum(_dot(qih_ref[h], kit), 0.0)
        sc = jnp.where(lane + ks <= row + qs, sc, NEG_INF)
        sc_ref[:, pl.ds(ks, LANES)] = _as_f32(_ordered_key(sc))
        return carry

    lax.fori_loop(0, blk + 1, score_block, 0)

    _topk_to_mask(sc_ref, blk + 1, tq, topk)

    _split_heads(q_ref, qh_ref)
    _flash_init(m_ref, l_ref, acc_ref)

    def block(kb, bias_off):
        ks = pl.multiple_of(kb * LANES, LANES)
        mask = sc_ref[:, pl.ds(ks, LANES)]
        for h in range(ATT_HEADS):
            j = h // 2
            s = _dot(qh_ref[h], kt_ref[j * LANES:(j + 1) * LANES, pl.ds(ks, LANES)]) + mask
            if bias_off is not None:
                s = s + bias_ref[h, :, bias_off:bias_off + LANES]
            _flash_update(h, s, v_ref[pl.ds(ks, LANES), j * LANES:(j + 1) * LANES], m_ref, l_ref, acc_ref)

    def far(kb, carry):
        block(kb, None)
        return carry

    lax.fori_loop(0, jnp.maximum(blk - 1, 0), far, 0)

    @pl.when(blk >= 1)
    def _():
        block(blk - 1, 0)

    block(blk, LANES)
    _flash_finish(o_ref, l_ref, acc_ref)


def _t5_bucket_table(max_dist):
    n = np.arange(max_dist + 1)
    nf = np.maximum(n, 1).astype(np.float32)
    scale = np.float32((T5_BUCKETS - T5_MAX_EXACT) / math.log(T5_MAX_DISTANCE / T5_MAX_EXACT))
    large = T5_MAX_EXACT + (np.log(nf / np.float32(T5_MAX_EXACT)) * scale).astype(np.int32)
    large = np.minimum(large, T5_BUCKETS - 1)
    return np.where(n < T5_MAX_EXACT, n, large).astype(np.int32)


def dsa_prompt_attend(q, qi, wi, kit2, kt, v, rel_bias, bsz, seq):
    tq = LANES
    hd = ATT_HEADS * ATT_DH
    nq = seq // tq
    topk = min(DSA_TOPK, seq // 4)
    buckets = _t5_bucket_table(seq)
    assert np.all(buckets[tq:] == buckets[tq])
    dist = np.arange(tq)[:, None] + tq - np.arange(2 * tq)[None, :]
    near = rel_bias[buckets[np.maximum(dist, 0)]] - rel_bias[buckets[tq]]
    near = jnp.transpose(near, (2, 0, 1)).astype(F32)
    once = pl.Buffered(1)
    return pl.pallas_call(
        functools.partial(_dsa_prompt_kernel, topk=topk),
        grid=(bsz, nq),
        in_specs=[pl.BlockSpec((tq, hd), lambda b_, i: (b_ * nq + i, 0)),
                  pl.BlockSpec((tq, IDX_HEADS * IDX_DIM), lambda b_, i: (b_ * nq + i, 0)),
                  pl.BlockSpec((tq, IDX_HEADS), lambda b_, i: (b_ * nq + i, 0)),
                  pl.BlockSpec((2 * IDX_DIM, seq), lambda b_, i: (0, b_), pipeline_mode=once),
                  pl.BlockSpec((hd, seq), lambda b_, i: (0, b_), pipeline_mode=once),
                  pl.BlockSpec((seq, hd), lambda b_, i: (b_, 0), pipeline_mode=once),
                  pl.BlockSpec((ATT_HEADS, tq, 2 * tq), lambda b_, i: (0, 0, 0), pipeline_mode=once)],
        out_specs=pl.BlockSpec((tq, hd), lambda b_, i: (b_ * nq + i, 0)),
        out_shape=jax.ShapeDtypeStruct((bsz * seq, hd), BF16),
        scratch_shapes=[pltpu.VMEM((tq, seq), F32), pltpu.VMEM((ATT_HEADS, tq, LANES), BF16),
                        pltpu.VMEM((IDX_HEADS, tq, LANES), BF16), pltpu.VMEM((ATT_HEADS, tq, 1), F32),
                        pltpu.VMEM((ATT_HEADS, tq, 1), F32), pltpu.VMEM((ATT_HEADS, tq, LANES), F32)],
        compiler_params=_cparams(2, 56),
        name="dsa_prompt",
    )(q, qi, wi, kit2, kt, v, near)


def _dsa_sample_score_kernel(pt_ref, qi_ref, wi_ref, kin_ref, *refs):
    pp = PAGES_PER_STEP
    ki_refs = refs[:pp]
    sc_ref, scn_ref = refs[pp:]
    qi = qi_ref[0]
    wi = wi_ref[0] * IDX_SCALE
    for p_i in range(pp):
        d = jnp.maximum(_dot_t(qi, ki_refs[p_i][0].astype(BF16)), 0.0)
        sc_ref[0, :, p_i * PAGE_SIZE:(p_i + 1) * PAGE_SIZE] = jnp.sum(wi * d, axis=0, keepdims=True)
    d_new = jnp.sum(qi.astype(F32) * kin_ref[0].astype(BF16).astype(F32), axis=-1, keepdims=True)
    s_new = jnp.sum(wi * jnp.maximum(d_new, 0.0), axis=0, keepdims=True)
    scn_ref[0] = jnp.broadcast_to(s_new, (1, LANES))


def dsa_sample_scores(page_table, qi, wi, ki_new, pool_ki):
    pp = PAGES_PER_STEP
    bsz = qi.shape[0]
    n_pages = page_table.shape[1]
    past = n_pages * PAGE_SIZE
    page = lambda p_i: pl.BlockSpec((1, PAGE_SIZE, IDX_DIM), lambda b_, j, pt, p_i=p_i: (pt[b_, j * pp + p_i], 0, 0))
    sc, scn = pl.pallas_call(
        _dsa_sample_score_kernel,
        grid_spec=pltpu.PrefetchScalarGridSpec(
            num_scalar_prefetch=1, grid=(bsz, n_pages // pp),
            in_specs=[pl.BlockSpec((1, IDX_HEADS, IDX_DIM), lambda b_, j, pt: (b_, 0, 0)),
                      pl.BlockSpec((1, IDX_HEADS, 1), lambda b_, j, pt: (b_, 0, 0)),
                      pl.BlockSpec((1, 1, IDX_DIM), lambda b_, j, pt: (b_, 0, 0))] + [page(p_i) for p_i in range(pp)],
            out_specs=[pl.BlockSpec((1, 1, pp * PAGE_SIZE), lambda b_, j, pt: (b_, 0, j)),
                       pl.BlockSpec((1, 1, LANES), lambda b_, j, pt: (b_, 0, 0))]),
        out_shape=[jax.ShapeDtypeStruct((bsz, 1, past), F32), jax.ShapeDtypeStruct((bsz, 1, LANES), F32)],
        compiler_params=_cparams(2, 16),
        name="dsa_sample_scores",
    )(page_table, qi.reshape(bsz, IDX_HEADS, IDX_DIM), wi.reshape(bsz, IDX_HEADS, 1),
      ki_new.reshape(bsz, 1, IDX_DIM), *([pool_ki] * pp))
    pad = jnp.full((bsz, LANES - 1), NEG_INF, F32)
    return jnp.concatenate([sc[:, 0], scn[:, 0, :1], pad], axis=1)


def _dsa_sample_select_kernel(sc_ref, mask_ref, *, topk):
    rows, width = sc_ref.shape
    mask_ref[...] = _as_f32(_ordered_key(sc_ref[...]))
    _topk_to_mask(mask_ref, width // LANES, rows, topk)


def dsa_sample_select(scores, topk):
    rows, width = scores.shape
    return pl.pallas_call(
        functools.partial(_dsa_sample_select_kernel, topk=topk),
        grid=(1,),
        in_specs=[pl.BlockSpec((rows, width), lambda i: (0, 0))],
        out_specs=pl.BlockSpec((rows, width), lambda i: (0, 0)),
        out_shape=jax.ShapeDtypeStruct((rows, width), F32),
        compiler_params=_cparams(1, 16),
        name="dsa_sample_select",
    )(scores)


def dsa_layer(xp, xs, g, w_in, w_out, rel_bias, pool_k, pool_v, pool_ki, page_table, bsz, seq):
    hd = DSA_HEADS * DSA_DH
    e_qi, e_ki = 3 * hd + IDX_HEADS * IDX_DIM, 3 * hd + IDX_HEADS * IDX_DIM + IDX_DIM
    wq = w_in[:, :hd] * (DSA_DH ** -0.5)
    wk, wv = w_in[:, hd:2 * hd], w_in[:, 2 * hd:3 * hd]
    wqi, wki, wwi = w_in[:, 3 * hd:e_qi], w_in[:, e_qi:e_ki], w_in[:, e_ki:]
    wki2 = jnp.concatenate([wki, wki], axis=1)
    q, k, kt, v, vb, qi, ki, kit2, wi = norm_proj(
        xp, g, [(wq, False, [BF16]), (wk, False, [F32]), (wk, True, [BF16]), (wv, False, [F32, BF16]),
                (wqi, False, [BF16]), (wki, False, [F32]), (wki2, True, [BF16]), (wwi, False, [F32])], tm=512)
    o = dsa_prompt_attend(q, qi, wi, kit2, kt, vb, rel_bias, bsz, seq)
    yp = out_proj(o, w_out, xp, tm=512)

    dbs = xs.shape[0]
    qs, ks, vs, qis, kis, wis = norm_proj(
        xs, g, [(wq, False, [BF16]), (wk, False, [F32]), (wv, False, [F32]), (wqi, False, [BF16]),
                (wki, False, [F32]), (wwi, False, [F32])], tm=dbs)
    past = page_table.shape[1] * PAGE_SIZE
    scores = dsa_sample_scores(page_table, qis, wis, kis, pool_ki)
    mask = dsa_sample_select(scores, min(DSA_TOPK, (past + 1) // 4))
    buckets = _t5_bucket_table(past)
    dist = np.maximum(past - np.arange(past + LANES), 0)
    bias = rel_bias[buckets[dist]].T.astype(F32)
    os_ = paged_attend("dsa", page_table, qs, ks, vs, pool_k, pool_v, (mask.reshape(dbs, 1, past + LANES), bias))
    ys = out_proj(os_, w_out, xs, tm=dbs)
    shp = lambda t, n: t.reshape(n, -1, DSA_HEADS, DSA_DH)
    return (yp, ys, (shp(k, bsz), shp(v, bsz), ki.reshape(bsz, seq, IDX_DIM)),
            (shp(ks, dbs), shp(vs, dbs), kis.reshape(dbs, 1, IDX_DIM)))


def _logf_rows_kernel(f_ref, b_ref, o_ref):
    o_ref[...] = _log_sigmoid(f_ref[...] + b_ref[...])


def fox_logf_rows(f, b_f):
    n, h = f.shape
    return pl.pallas_call(
        _logf_rows_kernel,
        grid=(1,),
        in_specs=[pl.BlockSpec((n, h), lambda i: (0, 0)), pl.BlockSpec((1, h), lambda i: (0, 0))],
        out_specs=pl.BlockSpec((n, h), lambda i: (0, 0)),
        out_shape=jax.ShapeDtypeStruct((n, h), F32),
        name="fox_logf_rows",
    )(f, b_f.reshape(1, h))


def kernel(x_prompt, x_sample, state_mlstm_C, state_mlstm_n, state_mlstm_m, cache_dsa_k, cache_dsa_v,
           cache_dsa_kidx, state_rglru_h, state_rglru_conv, cache_fox_k, cache_fox_v, cache_fox_logf, page_table,
           norm_mix, norm_mlp, norm_final, mlstm_w_in, mlstm_b_gate, mlstm_norm, mlstm_w_out, dsa_w_in, dsa_w_out,
           rel_bias, rglru_w_in, rglru_conv_w, rglru_conv_b, rglru_w_a, rglru_b_a, rglru_w_x, rglru_b_x, rglru_lam,
           rglru_w_out, fox_w_in, fox_b_f, fox_w_out, mlp_w1, mlp_w2):
    bsz, seq, d = x_prompt.shape
    dbs = x_sample.shape[0]
    assert x_sample.shape[1] == 1
    depth = norm_mix.shape[0]
    yp = x_prompt.reshape(bsz * seq, d)
    ys = x_sample.reshape(dbs, d)
    states = {kind: ([], []) for kind in range(4)}
    for layer in range(depth):
        kind, inst = layer % 4, layer // 4
        g = norm_mix[layer]
        if kind == 0:
            yp, ys, st_p, st_s = mlstm_layer(yp, ys, g, mlstm_w_in[inst], mlstm_b_gate[inst], mlstm_norm[inst],
                                             mlstm_w_out[inst], state_mlstm_C[inst], state_mlstm_n[inst],
                                             state_mlstm_m[inst], bsz, seq)
        elif kind == 1:
            yp, ys, st_p, st_s = dsa_layer(yp, ys, g, dsa_w_in[inst], dsa_w_out[inst], rel_bias, cache_dsa_k[inst],
                                           cache_dsa_v[inst], cache_dsa_kidx[inst], page_table, bsz, seq)
        elif kind == 2:
            yp, ys, st_p, st_s = rglru_layer(yp, ys, g, rglru_w_in[inst], rglru_conv_w[inst], rglru_conv_b[inst],
                                             rglru_w_a[inst], rglru_b_a[inst], rglru_w_x[inst], rglru_b_x[inst],
                                             rglru_lam[inst], rglru_w_out[inst], state_rglru_h[inst],
                                             state_rglru_conv[inst], bsz, seq)
        else:
            yp, ys, st_p, st_s = fox_layer(yp, ys, g, fox_w_in[inst], fox_b_f[inst], fox_w_out[inst],
                                           cache_fox_k[inst], cache_fox_v[inst], cache_fox_logf[inst], page_table,
                                           bsz, seq)
        states[kind][0].append(st_p)
        states[kind][1].append(st_s)
        last = layer == depth - 1
        yp = mlp(yp, norm_mlp[layer], mlp_w1[layer], mlp_w2[layer], norm_final, last, tm=1024, tf=512)
        ys = mlp(ys, norm_mlp[layer], mlp_w1[layer], mlp_w2[layer], norm_final, last, tm=dbs, tf=512)

    def stk(kind, group, j):
        return jnp.stack([entry[j] for entry in states[kind][group]])

    outs = [yp.reshape(bsz, seq, d), ys.reshape(dbs, 1, d)]
    for kind, n_leaves in ((0, 3), (1, 3), (2, 2), (3, 3)):
        for group in (0, 1):
            outs += [stk(kind, group, j) for j in range(n_leaves)]
    return tuple(outs)
```

```python
import functools
import math

import numpy as np
import jax
import jax.numpy as jnp
from jax import lax
from jax.experimental import pallas as pl
from jax.experimental.pallas import tpu as pltpu

F32 = jnp.float32
BF16 = jnp.bfloat16
I32 = jnp.int32
NEG_INF = float("-inf")

RMS_EPS = 1e-6
D_MODEL = 1024
PAGE_SIZE = 128

MLSTM_HEADS = 8
MLSTM_DQK = 64
MLSTM_DV = 128
MLSTM_CHUNK = 128

DSA_HEADS = 16
DSA_DH = 64
IDX_HEADS = 8
IDX_DIM = 64
DSA_TOPK = 256
IDX_SCALE = (IDX_DIM * IDX_HEADS) ** -0.5
T5_BUCKETS = 32
T5_MAX_EXACT = 16
T5_MAX_DISTANCE = 128

RG_WIDTH = 1280
RG_BLOCKS = 10
RG_BLOCK = 128
RG_CONV = 4
RG_C = 8.0

FOX_HEADS = 16
FOX_DH = 64
FOX_TQ = 256

MLP_TM = 1024
MLP_TF = 512

VMEM_BUDGET_BYTES = 56 * 1024 * 1024
KEY_NEG_INF = np.int32(-2139095041)
INT_MIN = np.int32(-2147483648)
INT_MAX = np.int32(2147483647)


def _cparams(n_grid, vmem_mb=32):
    return pltpu.CompilerParams(dimension_semantics=("arbitrary",) * n_grid,
                                vmem_limit_bytes=min(vmem_mb * 1024 * 1024, VMEM_BUDGET_BYTES))


def _log_sigmoid(x):
    return jnp.minimum(x, 0.0) - jnp.log1p(jnp.exp(-jnp.abs(x)))


def _rms(x, g):
    return x * lax.rsqrt(jnp.mean(x * x, axis=-1, keepdims=True) + RMS_EPS) * g


def _dot(a, b):
    return jnp.dot(a, b, preferred_element_type=F32)


def _dot_t(a, b):
    return lax.dot_general(a, b, (((1,), (1,)), ((), ())), preferred_element_type=F32)


def _norm_proj_kernel(x_ref, g_ref, *refs, plan):
    n_w = len(plan)
    w_refs = refs[:n_w]
    o_refs = refs[n_w:]
    xn = _rms(x_ref[...], g_ref[...]).astype(BF16)
    oi = 0
    for w_ref, (transposed, dtypes) in zip(w_refs, plan):
        o = _dot_t(w_ref[...], xn) if transposed else _dot(xn, w_ref[...])
        for dt in dtypes:
            o_refs[oi][...] = o.astype(dt)
            oi += 1


def norm_proj(x, g, segments, tm):
    n_rows, d = x.shape
    assert n_rows % tm == 0
    plan, w_args, w_specs, out_shapes, out_specs = [], [], [], [], []
    for w, transposed, dtypes in segments:
        n = w.shape[1]
        wb = w.astype(BF16)
        if transposed:
            wb = wb.T
            w_specs.append(pl.BlockSpec((n, d), lambda i: (0, 0)))
        else:
            w_specs.append(pl.BlockSpec((d, n), lambda i: (0, 0)))
        w_args.append(wb)
        plan.append((transposed, tuple(dtypes)))
        for dt in dtypes:
            if transposed:
                out_shapes.append(jax.ShapeDtypeStruct((n, n_rows), dt))
                out_specs.append(pl.BlockSpec((n, tm), lambda i: (0, i)))
            else:
                out_shapes.append(jax.ShapeDtypeStruct((n_rows, n), dt))
                out_specs.append(pl.BlockSpec((tm, n), lambda i: (i, 0)))
    return pl.pallas_call(
        functools.partial(_norm_proj_kernel, plan=tuple(plan)),
        grid=(n_rows // tm,),
        in_specs=[pl.BlockSpec((tm, d), lambda i: (i, 0)), pl.BlockSpec((1, d), lambda i: (0, 0))] + w_specs,
        out_specs=out_specs,
        out_shape=out_shapes,
        compiler_params=_cparams(1, 48),
        name="norm_proj",
    )(x, g.reshape(1, d), *w_args)


def _out_proj_kernel(a_ref, w_ref, r_ref, o_ref):
    o_ref[...] = r_ref[...] + _dot(a_ref[...], w_ref[...])


def out_proj(a, w, resid, tm):
    n_rows, k = a.shape
    d = w.shape[1]
    return pl.pallas_call(
        _out_proj_kernel,
        grid=(n_rows // tm,),
        in_specs=[pl.BlockSpec((tm, k), lambda i: (i, 0)), pl.BlockSpec((k, d), lambda i: (0, 0)),
                  pl.BlockSpec((tm, d), lambda i: (i, 0))],
        out_specs=pl.BlockSpec((tm, d), lambda i: (i, 0)),
        out_shape=jax.ShapeDtypeStruct((n_rows, d), F32),
        compiler_params=_cparams(1, 32),
        name="out_proj",
    )(a, w.astype(BF16), resid)


def _mlp_kernel(x_ref, g_ref, w1_ref, w2_ref, gf_ref, o_ref, xn_ref, acc_ref, *, final_norm):
    f = pl.program_id(1)

    @pl.when(f == 0)
    def _():
        xn_ref[...] = _rms(x_ref[...], g_ref[...]).astype(BF16)
        acc_ref[...] = jnp.zeros_like(acc_ref)

    h = jnp.maximum(_dot(xn_ref[...], w1_ref[...]), 0.0)
    acc_ref[...] += _dot((h * h).astype(BF16), w2_ref[...])

    @pl.when(f == pl.num_programs(1) - 1)
    def _():
        y = x_ref[...] + acc_ref[...]
        o_ref[...] = _rms(y, gf_ref[...]) if final_norm else y


def mlp(x, g, w1, w2, g_final, final_norm, tm, tf):
    n_rows, d = x.shape
    d_ff = w1.shape[1]
    return pl.pallas_call(
        functools.partial(_mlp_kernel, final_norm=final_norm),
        grid=(n_rows // tm, d_ff // tf),
        in_specs=[pl.BlockSpec((tm, d), lambda i, f: (i, 0)), pl.BlockSpec((1, d), lambda i, f: (0, 0)),
                  pl.BlockSpec((d, tf), lambda i, f: (0, f)), pl.BlockSpec((tf, d), lambda i, f: (f, 0)),
                  pl.BlockSpec((1, d), lambda i, f: (0, 0))],
        out_specs=pl.BlockSpec((tm, d), lambda i, f: (i, 0)),
        out_shape=jax.ShapeDtypeStruct((n_rows, d), F32),
        scratch_shapes=[pltpu.VMEM((tm, d), BF16), pltpu.VMEM((tm, d), F32)],
        compiler_params=_cparams(2, 48),
        name="mlp",
    )(x, g.reshape(1, d), w1.astype(BF16), w2.astype(BF16), g_final.reshape(1, d))


def _lane_cumsum(x):
    lane = lax.broadcasted_iota(I32, x.shape, 1)
    d = 1
    while d < x.shape[1]:
        x = x + jnp.where(lane >= d, pltpu.roll(x, d, axis=1), 0.0)
        d *= 2
    return x


def _lane_cummax(x):
    lane = lax.broadcasted_iota(I32, x.shape, 1)
    d = 1
    while d < x.shape[1]:
        x = jnp.maximum(x, jnp.where(lane >= d, pltpu.roll(x, d, axis=1), NEG_INF))
        d *= 2
    return x


def _mlstm_prompt_kernel(q_ref, kt_ref, v_ref, o_ref, gt_ref, bg_ref, ng_ref,
                         hh_ref, cst_ref, mst_ref, ct_ref, m_ref):
    c = pl.program_id(1)
    L = q_ref.shape[0]
    H, DQK, DV = MLSTM_HEADS, MLSTM_DQK, MLSTM_DV

    @pl.when(c == 0)
    def _():
        ct_ref[...] = jnp.zeros_like(ct_ref)
        m_ref[...] = jnp.full_like(m_ref, NEG_INF)

    gates = gt_ref[...] + bg_ref[...]
    ig = gates[:H]
    lf = _log_sigmoid(gates[H:])
    b = _lane_cumsum(lf)
    a = ig - b
    m0 = m_ref[...]
    big_m = jnp.maximum(m0, _lane_cummax(a))
    m = b + big_m
    cols = jnp.concatenate([big_m, m, jnp.zeros((128 - 2 * H, L), F32)], axis=0).T
    row_i = lax.broadcasted_iota(I32, (L, L), 0)
    col_i = lax.broadcasted_iota(I32, (L, L), 1)
    causal = col_i <= row_i
    ones_col = (lax.broadcasted_iota(I32, (L, DV), 1) == 0).astype(BF16)
    for h in range(H):
        q = q_ref[:, h * DQK:(h + 1) * DQK]
        kt = kt_ref[h * DQK:(h + 1) * DQK, :]
        v_ext = jnp.concatenate([v_ref[:, h * DV:(h + 1) * DV], ones_col], axis=1)
        big_m_col = cols[:, h:h + 1]
        m_col = cols[:, H + h:H + h + 1]
        w = jnp.exp(jnp.where(causal, a[h:h + 1, :] - big_m_col, NEG_INF))
        w_inter = jnp.exp(m0[h:h + 1, :] - big_m_col)
        s = _dot(q, kt) * w
        ct = ct_ref[h]
        tot = w_inter * _dot(q, ct.astype(BF16)) + _dot(s.astype(BF16), v_ext)
        num = tot[:, :DV]
        den = tot[:, DV:DV + 1]
        hv = num / jnp.maximum(jnp.abs(den), jnp.exp(-m_col))
        hv = _rms(hv, ng_ref[h:h + 1, :])
        gate = jax.nn.sigmoid(o_ref[:, h * DV:(h + 1) * DV])
        hh_ref[:, h * DV:(h + 1) * DV] = (hv * gate).astype(hh_ref.dtype)
        g_row = w[L - 1:L, :]
        decay = w_inter[L - 1:L, :]
        ct_ref[h] = decay * ct + _dot((kt.astype(F32) * g_row).astype(BF16), v_ext)
    m_ref[...] = m[:, L - 1:L]

    @pl.when(c == pl.num_programs(1) - 1)
    def _():
        cst_ref[0] = ct_ref[...]
        mst_ref[0] = m_ref[...]


def mlstm_prompt(q, kt, v, o, gt, b_gate, norm_g, bsz, seq):
    H, DQK, DV, L = MLSTM_HEADS, MLSTM_DQK, MLSTM_DV, MLSTM_CHUNK
    nc = seq // L
    n_rows = bsz * seq
    row = lambda b_, c_: (b_ * nc + c_, 0)
    col = lambda b_, c_: (0, b_ * nc + c_)
    fixed = lambda b_, c_: (0, 0)
    hh, cst, mst = pl.pallas_call(
        _mlstm_prompt_kernel,
        grid=(bsz, nc),
        in_specs=[pl.BlockSpec((L, H * DQK), row), pl.BlockSpec((H * DQK, L), col),
                  pl.BlockSpec((L, H * DV), row), pl.BlockSpec((L, H * DV), row),
                  pl.BlockSpec((2 * H, L), col), pl.BlockSpec((2 * H, 1), fixed),
                  pl.BlockSpec((H, DV), fixed)],
        out_specs=[pl.BlockSpec((L, H * DV), row),
                   pl.BlockSpec((1, H, DQK, 2 * DV), lambda b_, c_: (b_, 0, 0, 0)),
                   pl.BlockSpec((1, H, 1), lambda b_, c_: (b_, 0, 0))],
        out_shape=[jax.ShapeDtypeStruct((n_rows, H * DV), BF16),
                   jax.ShapeDtypeStruct((bsz, H, DQK, 2 * DV), F32),
                   jax.ShapeDtypeStruct((bsz, H, 1), F32)],
        scratch_shapes=[pltpu.VMEM((H, DQK, 2 * DV), F32), pltpu.VMEM((H, 1), F32)],
        compiler_params=_cparams(2, 32),
        name="mlstm_prompt",
    )(q, kt, v, o, gt, b_gate.reshape(2 * H, 1), norm_g)
    c_new = jnp.swapaxes(cst[..., :DV], 2, 3)
    n_new = cst[..., DV]
    return hh, c_new, n_new, mst[..., 0]


def _mlstm_sample_kernel(q_ref, k_ref, v_ref, vt_ref, o_ref, g_ref, bg_ref, ng_ref, c0_ref, n0_ref, m0_ref,
                         hh_ref, c_ref, n_ref, m_ref):
    H = MLSTM_HEADS
    gates = g_ref[0] + bg_ref[...]
    ig = gates[:H]
    lf = _log_sigmoid(gates[H:])
    m0 = m0_ref[0]
    m = jnp.maximum(lf + m0, ig)
    w_inter = jnp.exp(lf + m0 - m)
    w_intra = jnp.exp(ig - m)
    qb = q_ref[0]
    q32 = qb.astype(F32)
    k = k_ref[0]
    v = v_ref[0]
    s = jnp.sum(q32 * k.astype(BF16).astype(F32), axis=-1, keepdims=True) * w_intra
    n0 = n0_ref[0]
    den = w_inter * jnp.sum(q32 * n0.astype(BF16).astype(F32), axis=-1, keepdims=True) + s
    row_i = lax.broadcasted_iota(I32, (H, MLSTM_DV), 0)
    inter = jnp.zeros((H, MLSTM_DV), F32)
    for h in range(H):
        c0 = c0_ref[0, h]
        inter = jnp.where(row_i == h, _dot_t(qb, c0.astype(BF16)), inter)
        c_ref[0, h] = w_inter[h:h + 1, :] * c0 + (w_intra[h:h + 1, :] * vt_ref[0, :, h:h + 1]) * k[h:h + 1, :]
    num = w_inter * inter + s * v.astype(BF16).astype(F32)
    hv = num / jnp.maximum(jnp.abs(den), jnp.exp(-m))
    hv = _rms(hv, ng_ref[...])
    hh_ref[0] = (hv * jax.nn.sigmoid(o_ref[0])).astype(hh_ref.dtype)
    n_ref[0] = w_inter * n0 + w_intra * k
    m_ref[0] = m


def mlstm_sample(q, k, v, o, gates, b_gate, norm_g, c0, n0, m0):
    H, DQK, DV = MLSTM_HEADS, MLSTM_DQK, MLSTM_DV
    bsz = q.shape[0]
    q3 = q.reshape(bsz, H, DQK)
    k3 = k.reshape(bsz, H, DQK)
    v3 = v.reshape(bsz, H, DV)
    vt = jnp.swapaxes(v3, 1, 2)
    b3 = lambda *blk: pl.BlockSpec((1,) + blk, lambda i: (i,) + (0,) * len(blk))
    fixed = lambda *blk: pl.BlockSpec(blk, lambda i: (0,) * len(blk))
    hh, c_new, n_new, m_new = pl.pallas_call(
        _mlstm_sample_kernel,
        grid=(bsz,),
        in_specs=[b3(H, DQK), b3(H, DQK), b3(H, DV), b3(DV, H), b3(H, DV), b3(2 * H, 1), fixed(2 * H, 1),
                  fixed(H, DV), b3(H, DV, DQK), b3(H, DQK), b3(H, 1)],
        out_specs=[b3(H, DV), b3(H, DV, DQK), b3(H, DQK), b3(H, 1)],
        out_shape=[jax.ShapeDtypeStruct((bsz, H, DV), BF16), jax.ShapeDtypeStruct((bsz, H, DV, DQK), F32),
                   jax.ShapeDtypeStruct((bsz, H, DQK), F32), jax.ShapeDtypeStruct((bsz, H, 1), F32)],
        compiler_params=_cparams(1, 16),
        name="mlstm_sample",
    )(q3, k3, v3, vt, o.reshape(bsz, H, DV), gates.reshape(bsz, 2 * H, 1), b_gate.reshape(2 * H, 1), norm_g,
      c0, n0, m0.reshape(bsz, H, 1))
    return hh.reshape(bsz, H * DV), c_new, n_new, m_new[..., 0]


def mlstm_layer(xp, xs, g, w_in, b_gate, norm_g, w_out, c0, n0, m0, bsz, seq):
    H, DQK, DV = MLSTM_HEADS, MLSTM_DQK, MLSTM_DV
    e_q, e_k, e_v, e_o = H * DQK, 2 * H * DQK, 2 * H * DQK + H * DV, 2 * H * DQK + 2 * H * DV
    wq = w_in[:, :e_q] * (DQK ** -0.5)
    wk, wv, wo, wg = w_in[:, e_q:e_k], w_in[:, e_k:e_v], w_in[:, e_v:e_o], w_in[:, e_o:]
    q, kt, v, o, gt = norm_proj(xp, g, [(wq, False, [BF16]), (wk, True, [BF16]), (wv, False, [BF16]),
                                        (wo, False, [F32]), (wg, True, [F32])], tm=512)
    hh, c_p, n_p, m_p = mlstm_prompt(q, kt, v, o, gt, b_gate, norm_g, bsz, seq)
    yp = out_proj(hh, w_out, xp, tm=512)
    qs, ks, vs, os_, gs = norm_proj(xs, g, [(wq, False, [BF16]), (wk, False, [F32]), (wv, False, [F32]),
                                            (wo, False, [F32]), (wg, False, [F32])], tm=xs.shape[0])
    hhs, c_s, n_s, m_s = mlstm_sample(qs, ks, vs, os_, gs, b_gate, norm_g, c0, n0, m0)
    ys = out_proj(hhs, w_out, xs, tm=xs.shape[0])
    return yp, ys, (c_p, n_p, m_p), (c_s, n_s, m_s)


def _softplus(x):
    return jnp.maximum(x, 0.0) + jnp.log1p(jnp.exp(-jnp.abs(x)))


def _expm1(x):
    u = jnp.exp(x)
    um1 = u - 1.0
    return jnp.where(um1 == 0.0, x, jnp.where(um1 == -1.0, -1.0, um1 * x / jnp.log(u)))


def _rglru_gates(conv, n, wa_ref, wx_ref, ba_ref, bx_ref, lam_ref):
    sl = slice(n * RG_BLOCK, (n + 1) * RG_BLOCK)
    cb = conv.astype(BF16)
    r = jax.nn.sigmoid(_dot(cb, wa_ref[n]) + ba_ref[:, sl])
    i = jax.nn.sigmoid(_dot(cb, wx_ref[n]) + bx_ref[:, sl])
    log_a = -RG_C * r * _softplus(-lam_ref[:, sl])
    a = jnp.exp(log_a)
    u = jnp.sqrt(-_expm1(2.0 * log_a)) * (i * conv)
    return a, u


def _rglru_prompt_kernel(gate_ref, xb_ref, cw_ref, cb_ref, wa_ref, wx_ref, ba_ref, bx_ref, lam_ref,
                         y_ref, h_ref, tail_ref, xpad_ref, hc_ref):
    c = pl.program_id(1)
    T = xb_ref.shape[0]

    @pl.when(c == 0)
    def _():
        xpad_ref[0:8, :] = jnp.zeros((8, RG_WIDTH), F32)
        hc_ref[...] = jnp.zeros_like(hc_ref)

    xpad_ref[8:8 + T, :] = xb_ref[...]
    row = lax.broadcasted_iota(I32, (T, RG_BLOCK), 0)
    for n in range(RG_BLOCKS):
        sl = slice(n * RG_BLOCK, (n + 1) * RG_BLOCK)
        conv = cb_ref[:, sl]
        for j in range(RG_CONV):
            conv = conv + cw_ref[j:j + 1, sl] * xpad_ref[8 - (RG_CONV - 1) + j:8 - (RG_CONV - 1) + j + T, sl]
        a, u = _rglru_gates(conv, n, wa_ref, wx_ref, ba_ref, bx_ref, lam_ref)
        d = 1
        while d < T:
            keep = row >= d
            a_sh = jnp.where(keep, pltpu.roll(a, d, axis=0), 1.0)
            u_sh = jnp.where(keep, pltpu.roll(u, d, axis=0), 0.0)
            u = a * u_sh + u
            a = a * a_sh
            d *= 2
        h = a * hc_ref[:, sl] + u
        hc_ref[:, sl] = h[T - 1:T, :]
        y_ref[:, sl] = (h * jax.nn.gelu(gate_ref[:, sl])).astype(y_ref.dtype)
    xpad_ref[0:8, :] = xb_ref[T - 8:T, :]

    @pl.when(c == pl.num_programs(1) - 1)
    def _():
        h_ref[0] = hc_ref[...]
        tail_ref[0] = xb_ref[T - 8:T, :]


def rglru_prompt(gate, xb, conv_w, conv_b, w_a, b_a, w_x, b_x, lam, bsz, seq, tt):
    W = RG_WIDTH
    nc = seq // tt
    row = lambda b_, c_: (b_ * nc + c_, 0)
    fixed2 = lambda b_, c_: (0, 0)
    fixed3 = lambda b_, c_: (0, 0, 0)
    y, h, tail = pl.pallas_call(
        _rglru_prompt_kernel,
        grid=(bsz, nc),
        in_specs=[pl.BlockSpec((tt, W), row), pl.BlockSpec((tt, W), row),
                  pl.BlockSpec((RG_CONV, W), fixed2), pl.BlockSpec((1, W), fixed2),
                  pl.BlockSpec((RG_BLOCKS, RG_BLOCK, RG_BLOCK), fixed3),
                  pl.BlockSpec((RG_BLOCKS, RG_BLOCK, RG_BLOCK), fixed3),
                  pl.BlockSpec((1, W), fixed2), pl.BlockSpec((1, W), fixed2), pl.BlockSpec((1, W), fixed2)],
        out_specs=[pl.BlockSpec((tt, W), row), pl.BlockSpec((1, 1, W), lambda b_, c_: (b_, 0, 0)),
                   pl.BlockSpec((1, 8, W), lambda b_, c_: (b_, 0, 0))],
        out_shape=[jax.ShapeDtypeStruct((bsz * seq, W), BF16), jax.ShapeDtypeStruct((bsz, 1, W), F32),
                   jax.ShapeDtypeStruct((bsz, 8, W), F32)],
        scratch_shapes=[pltpu.VMEM((tt + 8, W), F32), pltpu.VMEM((1, W), F32)],
        compiler_params=_cparams(2, 32),
        name="rglru_prompt",
    )(gate, xb, conv_w, conv_b.reshape(1, W), w_a.astype(BF16), w_x.astype(BF16),
      b_a.reshape(1, W), b_x.reshape(1, W), lam.reshape(1, W))
    return y, h[:, 0], tail[:, 8 - (RG_CONV - 1):]


def _rglru_sample_kernel(gate_ref, xb_ref, c0_ref, c1_ref, c2_ref, h0_ref, cw_ref, cb_ref, wa_ref, wx_ref,
                         ba_ref, bx_ref, lam_ref, y_ref, h_ref):
    taps = (c0_ref, c1_ref, c2_ref, xb_ref)
    for n in range(RG_BLOCKS):
        sl = slice(n * RG_BLOCK, (n + 1) * RG_BLOCK)
        conv = cb_ref[:, sl]
        for j in range(RG_CONV):
            conv = conv + cw_ref[j:j + 1, sl] * taps[j][:, sl]
        a, u = _rglru_gates(conv, n, wa_ref, wx_ref, ba_ref, bx_ref, lam_ref)
        h = a * h0_ref[:, sl] + u
        h_ref[:, sl] = h
        y_ref[:, sl] = (h * jax.nn.gelu(gate_ref[:, sl])).astype(y_ref.dtype)


def rglru_sample(gate, xb, conv0, h0, conv_w, conv_b, w_a, b_a, w_x, b_x, lam):
    W = RG_WIDTH
    bsz = gate.shape[0]
    full2 = lambda r: pl.BlockSpec((r, W), lambda i: (0, 0))
    full3 = pl.BlockSpec((RG_BLOCKS, RG_BLOCK, RG_BLOCK), lambda i: (0, 0, 0))
    y, h = pl.pallas_call(
        _rglru_sample_kernel,
        grid=(1,),
        in_specs=[full2(bsz)] * 6 + [full2(RG_CONV), full2(1), full3, full3, full2(1), full2(1), full2(1)],
        out_specs=[full2(bsz), full2(bsz)],
        out_shape=[jax.ShapeDtypeStruct((bsz, W), BF16), jax.ShapeDtypeStruct((bsz, W), F32)],
        compiler_params=_cparams(1, 16),
        name="rglru_sample",
    )(gate, xb, conv0[:, 0], conv0[:, 1], conv0[:, 2], h0, conv_w, conv_b.reshape(1, W),
      w_a.astype(BF16), w_x.astype(BF16), b_a.reshape(1, W), b_x.reshape(1, W), lam.reshape(1, W))
    new_conv = jnp.concatenate([conv0[:, 1:], xb[:, None, :]], axis=1)
    return y, h, new_conv


def rglru_layer(xp, xs, g, w_in, conv_w, conv_b, w_a, b_a, w_x, b_x, lam, w_out, h0, conv0, bsz, seq):
    W = RG_WIDTH
    segs = [(w_in[:, :W], False, [F32]), (w_in[:, W:], False, [F32])]
    gate, xb = norm_proj(xp, g, segs, tm=512)
    y, h_p, conv_p = rglru_prompt(gate, xb, conv_w, conv_b, w_a, b_a, w_x, b_x, lam, bsz, seq, tt=256)
    yp = out_proj(y, w_out, xp, tm=512)
    gate_s, xb_s = norm_proj(xs, g, segs, tm=xs.shape[0])
    y_s, h_s, conv_s = rglru_sample(gate_s, xb_s, conv0, h0, conv_w, conv_b, w_a, b_a, w_x, b_x, lam)
    ys = out_proj(y_s, w_out, xs, tm=xs.shape[0])
    return yp, ys, (h_p, conv_p), (h_s, conv_s)


ATT_HEADS = 16
ATT_DH = 64
ATT_PAIRS = ATT_HEADS // 2
LANES = 128


def _split_heads(q_ref, qh_ref):
    tq = q_ref.shape[0]
    low_half = lax.broadcasted_iota(I32, (tq, LANES), 1) < ATT_DH
    for j in range(ATT_PAIRS):
        qp = q_ref[:, j * LANES:(j + 1) * LANES]
        zero = jnp.zeros_like(qp)
        qh_ref[2 * j] = jnp.where(low_half, qp, zero)
        qh_ref[2 * j + 1] = jnp.where(low_half, zero, qp)


def _lane_fold(x, op):
    out = x[:, :LANES]
    for c in range(1, x.shape[1] // LANES):
        out = op(out, x[:, c * LANES:(c + 1) * LANES])
    return out


def _flash_update(h, s, v_pair, m_ref, l_ref, acc_ref, may_be_empty):
    reps = s.shape[1] // LANES
    m_old = m_ref[h]
    m_new = jnp.maximum(m_old, jnp.max(_lane_fold(s, jnp.maximum), axis=-1, keepdims=True))
    m_ref[h] = m_new
    if may_be_empty:
        m_new = jnp.where(m_new == NEG_INF, 0.0, m_new)
    alpha = jnp.exp(m_old - m_new)
    p = jnp.exp(s - jnp.concatenate([m_new] * reps, axis=1))
    l_ref[h] = alpha * l_ref[h] + _lane_fold(p, jnp.add)
    acc_ref[h] = alpha * acc_ref[h] + _dot(p.astype(BF16), v_pair)


def _flash_finish(o_ref, l_ref, acc_ref):
    tq = o_ref.shape[0]
    low_half = lax.broadcasted_iota(I32, (tq, LANES), 1) < ATT_DH
    for j in range(ATT_PAIRS):
        lo = acc_ref[2 * j] / jnp.sum(l_ref[2 * j], axis=-1, keepdims=True)
        hi = acc_ref[2 * j + 1] / jnp.sum(l_ref[2 * j + 1], axis=-1, keepdims=True)
        o_ref[:, j * LANES:(j + 1) * LANES] = jnp.where(low_half, lo, hi).astype(o_ref.dtype)


def _flash_init(m_ref, l_ref, acc_ref):
    m_ref[...] = jnp.full_like(m_ref, NEG_INF)
    l_ref[...] = jnp.zeros_like(l_ref)
    acc_ref[...] = jnp.zeros_like(acc_ref)


def _fox_gate_kernel(ft_ref, f_ref, bft_ref, bf_ref, ct_ref, logf_ref, carry_ref):
    c = pl.program_id(1)
    ts = ft_ref.shape[1]

    @pl.when(c == 0)
    def _():
        carry_ref[...] = jnp.zeros_like(carry_ref)

    cs = _lane_cumsum(_log_sigmoid(ft_ref[...] + bft_ref[...])) + carry_ref[...]
    ct_ref[...] = cs
    carry_ref[...] = cs[:, ts - 1:ts]
    logf_ref[...] = _log_sigmoid(f_ref[...] + bf_ref[...])


def fox_gates(ft, f, b_f, bsz, seq, ts):
    H = FOX_HEADS
    nc = seq // ts
    return pl.pallas_call(
        _fox_gate_kernel,
        grid=(bsz, nc),
        in_specs=[pl.BlockSpec((H, ts), lambda b_, c_: (0, b_ * nc + c_)),
                  pl.BlockSpec((ts, H), lambda b_, c_: (b_ * nc + c_, 0)),
                  pl.BlockSpec((H, 1), lambda b_, c_: (0, 0)), pl.BlockSpec((1, H), lambda b_, c_: (0, 0))],
        out_specs=[pl.BlockSpec((H, ts), lambda b_, c_: (0, b_ * nc + c_)),
                   pl.BlockSpec((ts, H), lambda b_, c_: (b_ * nc + c_, 0))],
        out_shape=[jax.ShapeDtypeStruct(ft.shape, F32), jax.ShapeDtypeStruct(f.shape, F32)],
        scratch_shapes=[pltpu.VMEM((H, 1), F32)],
        compiler_params=_cparams(2, 16),
        name="fox_gates",
    )(ft, f, b_f.reshape(H, 1), b_f.reshape(1, H))


def _fox_prompt_kernel(q_ref, kt_ref, v_ref, ct_ref, o_ref, qh_ref, m_ref, l_ref, acc_ref):
    qi = pl.program_id(1)
    tq = q_ref.shape[0]
    qs = pl.multiple_of(qi * tq, tq)
    _split_heads(q_ref, qh_ref)
    _flash_init(m_ref, l_ref, acc_ref)
    c_q0 = ct_ref[:, pl.ds(qs, LANES)][:, 0:1]
    causal = (lax.broadcasted_iota(I32, (tq, tq), 1) <= lax.broadcasted_iota(I32, (tq, tq), 0))

    def block(kb, diagonal):
        ks = pl.multiple_of(kb * tq, tq)
        decay = c_q0 - ct_ref[:, pl.ds(ks, tq)]
        for h in range(ATT_HEADS):
            j = h // 2
            s = _dot(qh_ref[h], kt_ref[j * LANES:(j + 1) * LANES, pl.ds(ks, tq)]) + decay[h:h + 1, :]
            if diagonal:
                s = jnp.where(causal, s, NEG_INF)
            _flash_update(h, s, v_ref[pl.ds(ks, tq), j * LANES:(j + 1) * LANES], m_ref, l_ref, acc_ref, False)

    def far(kb, carry):
        block(kb, False)
        return carry

    lax.fori_loop(0, qi, far, 0)
    block(qi, True)
    _flash_finish(o_ref, l_ref, acc_ref)


def fox_prompt_attend(q, kt, v, ct, bsz, seq, tq):
    hd = ATT_HEADS * ATT_DH
    nq = seq // tq
    once = pl.Buffered(1)
    return pl.pallas_call(
        _fox_prompt_kernel,
        grid=(bsz, nq),
        in_specs=[pl.BlockSpec((tq, hd), lambda b_, i: (b_ * nq + i, 0)),
                  pl.BlockSpec((hd, seq), lambda b_, i: (0, b_), pipeline_mode=once),
                  pl.BlockSpec((seq, hd), lambda b_, i: (b_, 0), pipeline_mode=once),
                  pl.BlockSpec((ATT_HEADS, seq), lambda b_, i: (0, b_), pipeline_mode=once)],
        out_specs=pl.BlockSpec((tq, hd), lambda b_, i: (b_ * nq + i, 0)),
        out_shape=jax.ShapeDtypeStruct((bsz * seq, hd), BF16),
        scratch_shapes=[pltpu.VMEM((ATT_HEADS, tq, LANES), BF16), pltpu.VMEM((ATT_HEADS, tq, LANES), F32),
                        pltpu.VMEM((ATT_HEADS, tq, LANES), F32), pltpu.VMEM((ATT_HEADS, tq, LANES), F32)],
        compiler_params=_cparams(2, 56),
        name="fox_prompt",
    )(q, kt, v, ct)


PAGES_PER_STEP = 8


def _paged_attend_kernel(pt_ref, qb_ref, qr_ref, kn_ref, vn_ref, *refs, mode):
    pp = PAGES_PER_STEP
    k_refs, v_refs = refs[:pp], refs[pp:2 * pp]
    refs = refs[2 * pp:]
    if mode == "fox":
        lf_refs, lfn_ref = refs[:pp], refs[pp]
        o_ref, m_ref, l_ref, acc_ref, cc_ref = refs[pp + 1:]
    else:
        mask_ref, maskn_ref, bias_ref, biasn_ref = refs[:4]
        o_ref, m_ref, l_ref, acc_ref = refs[4:]
    j = pl.program_id(1)
    H = ATT_HEADS

    @pl.when(j == 0)
    def _():
        _flash_init(m_ref, l_ref, acc_ref)
        if mode == "fox":
            cc_ref[...] = jnp.zeros_like(cc_ref)

    head_row = lax.broadcasted_iota(I32, (H, PAGE_SIZE), 0)

    def softmax_step(s):
        m_old = m_ref[...]
        m_new = jnp.maximum(m_old, jnp.max(s, axis=-1, keepdims=True))
        m_ref[...] = m_new
        m_safe = jnp.where(m_new == NEG_INF, 0.0, m_new)
        alpha = jnp.exp(m_old - m_safe)
        p = jnp.exp(s - m_safe)
        l_ref[...] = alpha * l_ref[...] + jnp.sum(p, axis=-1, keepdims=True)
        return alpha, p

    logits = []
    cc = cc_ref[...] if mode == "fox" else None
    for p_i in range(pp):
        s = jnp.zeros((H, PAGE_SIZE), F32)
        for h in range(H):
            part = jnp.sum(k_refs[p_i][0, h] * qb_ref[0, h], axis=0, keepdims=True)
            s = jnp.where(head_row == h, part, s)
        if mode == "fox":
            c = _lane_cumsum(lf_refs[p_i][0]) + cc
            cc = c[:, PAGE_SIZE - 1:PAGE_SIZE]
            s = s - c
        else:
            sl = slice(p_i * PAGE_SIZE, (p_i + 1) * PAGE_SIZE)
            s = s + mask_ref[0][:, sl] + bias_ref[:, sl]
        logits.append(s)
    if mode == "fox":
        cc_ref[...] = cc
    alpha, p = softmax_step(jnp.concatenate(logits, axis=1))
    for h in range(H):
        a = acc_ref[h] * alpha[h:h + 1, :]
        for p_i in range(pp):
            a = a + v_refs[p_i][0, h] * p[h:h + 1, p_i * PAGE_SIZE:(p_i + 1) * PAGE_SIZE]
        acc_ref[h] = a

    @pl.when(j == pl.num_programs(1) - 1)
    def _():
        s = jnp.sum(qr_ref[0] * kn_ref[0], axis=-1, keepdims=True)
        if mode == "fox":
            s = s - (cc_ref[...] + lfn_ref[0])
        else:
            s = s + maskn_ref[0][:, 0:1] + biasn_ref[:, 0:1]
        alpha, p = softmax_step(s)
        out_row = lax.broadcasted_iota(I32, (H, LANES), 0)
        out = jnp.zeros((H, LANES), F32)
        pad = jnp.zeros((LANES - ATT_DH, PAGE_SIZE), F32)
        for h in range(H):
            a_t = jnp.concatenate([acc_ref[h] * alpha[h:h + 1, :], pad], axis=0).T
            out = jnp.where(out_row == h, jnp.sum(a_t, axis=0, keepdims=True), out)
        out = (out[:, :ATT_DH] + p * vn_ref[0]) / l_ref[...]
        o_ref[0] = out.astype(o_ref.dtype)


def paged_attend(mode, page_table, q, k_new, v_new, pool_k, pool_v, extras):
    pp = PAGES_PER_STEP
    bsz, hd = q.shape
    H, DH = ATT_HEADS, ATT_DH
    n_pages = page_table.shape[1]
    past = n_pages * PAGE_SIZE
    assert n_pages % pp == 0
    pool_kt = jnp.transpose(pool_k, (0, 2, 3, 1))
    pool_vt = jnp.transpose(pool_v, (0, 2, 3, 1))
    q3 = q.astype(F32).reshape(bsz, H, DH)
    q_lanes = jnp.broadcast_to(q3[..., None], (bsz, H, DH, PAGE_SIZE))
    row3 = pl.BlockSpec((1, H, DH), lambda b_, j, pt: (b_, 0, 0))
    page = lambda p_i, *blk: pl.BlockSpec((1,) + blk, lambda b_, j, pt, p_i=p_i: (pt[b_, j * pp + p_i],) + (0,) * len(blk))
    in_specs = [pl.BlockSpec((1, H, DH, PAGE_SIZE), lambda b_, j, pt: (b_, 0, 0, 0)), row3, row3, row3]
    in_specs += [page(p_i, H, DH, PAGE_SIZE) for p_i in range(pp)] * 2
    args = [q_lanes, q3, k_new.reshape(bsz, H, DH), v_new.reshape(bsz, H, DH)]
    args += [pool_kt] * pp + [pool_vt] * pp
    scratch = [pltpu.VMEM((H, 1), F32), pltpu.VMEM((H, 1), F32), pltpu.VMEM((H, DH, PAGE_SIZE), F32)]
    if mode == "fox":
        pool_lft, lf_new = extras
        in_specs += [page(p_i, ATT_HEADS, PAGE_SIZE) for p_i in range(pp)]
        in_specs += [pl.BlockSpec((1, ATT_HEADS, 1), lambda b_, j, pt: (b_, 0, 0))]
        args += [pool_lft] * pp + [lf_new]
        scratch += [pltpu.VMEM((ATT_HEADS, 1), F32)]
    else:
        mask, bias = extras
        in_specs += [pl.BlockSpec((1, 1, pp * PAGE_SIZE), lambda b_, j, pt: (b_, 0, j)),
                     pl.BlockSpec((1, 1, LANES), lambda b_, j, pt: (b_, 0, past // LANES)),
                     pl.BlockSpec((ATT_HEADS, pp * PAGE_SIZE), lambda b_, j, pt: (0, j)),
                     pl.BlockSpec((ATT_HEADS, LANES), lambda b_, j, pt: (0, past // LANES))]
        args += [mask, mask, bias, bias]
    out = pl.pallas_call(
        functools.partial(_paged_attend_kernel, mode=mode),
        grid_spec=pltpu.PrefetchScalarGridSpec(
            num_scalar_prefetch=1, grid=(bsz, n_pages // pp), in_specs=in_specs,
            out_specs=pl.BlockSpec((1, H, DH), lambda b_, j, pt: (b_, 0, 0)), scratch_shapes=scratch),
        out_shape=jax.ShapeDtypeStruct((bsz, H, DH), BF16),
        compiler_params=_cparams(2, 48),
        name="paged_attend_" + mode,
    )(page_table, *args)
    return out.reshape(bsz, hd)


def fox_layer(xp, xs, g, w_in, b_f, w_out, pool_k, pool_v, pool_logf, page_table, bsz, seq):
    hd = FOX_HEADS * FOX_DH
    wq = w_in[:, :hd] * (FOX_DH ** -0.5)
    wk, wv, wf = w_in[:, hd:2 * hd], w_in[:, 2 * hd:3 * hd], w_in[:, 3 * hd:]
    q, k, kt, v, vb, f, ft = norm_proj(xp, g, [(wq, False, [BF16]), (wk, False, [F32]), (wk, True, [BF16]),
                                               (wv, False, [F32, BF16]), (wf, False, [F32]), (wf, True, [F32])],
                                       tm=512)
    ct, logf = fox_gates(ft, f, b_f, bsz, seq, ts=min(seq, 2048))
    o = fox_prompt_attend(q, kt, vb, ct, bsz, seq, tq=FOX_TQ)
    yp = out_proj(o, w_out, xp, tm=512)
    dbs = xs.shape[0]
    qs, ks, vs, fs = norm_proj(xs, g, [(wq, False, [BF16]), (wk, False, [F32]), (wv, False, [F32]),
                                       (wf, False, [F32])], tm=dbs)
    lf_s = fox_logf_rows(fs, b_f)
    os_ = paged_attend("fox", page_table, qs, ks, vs, pool_k, pool_v,
                       (jnp.swapaxes(pool_logf, 1, 2), lf_s.reshape(dbs, FOX_HEADS, 1)))
    ys = out_proj(os_, w_out, xs, tm=dbs)
    shp = lambda t, n: t.reshape(n, -1, FOX_HEADS, FOX_DH)
    return (yp, ys, (shp(k, bsz), shp(v, bsz), logf.reshape(bsz, seq, FOX_HEADS)),
            (shp(ks, dbs), shp(vs, dbs), lf_s.reshape(dbs, 1, FOX_HEADS)))


def _ordered_key(x):
    b = lax.bitcast_convert_type(x + 0.0, I32)
    return b ^ ((b >> 31) & INT_MAX)


def _as_f32(x):
    return lax.bitcast_convert_type(x, F32)


def _as_i32(x):
    return lax.bitcast_convert_type(x, I32)


def _topk_to_mask(sc_ref, n_blocks, width, rows, topk):
    lane = lax.broadcasted_iota(I32, (rows, width), 1)
    k_f = float(topk)

    def count(pred):
        def body(kb, acc):
            ks = pl.multiple_of(kb * width, width)
            hit = jnp.where(pred(_as_i32(sc_ref[:, pl.ds(ks, width)]), ks), 1.0, 0.0)
            return acc + _lane_fold(hit, jnp.add)
        acc = lax.fori_loop(0, n_blocks, body, jnp.zeros((rows, LANES), F32))
        return jnp.sum(acc, axis=-1, keepdims=True)

    t0 = jnp.where(count(lambda key, ks: key >= 0) >= k_f, 0, INT_MIN).astype(I32)

    def bit_step(i, t):
        cand = t | (1 << (30 - i))
        return jnp.where(count(lambda key, ks: key >= cand) >= k_f, cand, t)

    t = lax.fori_loop(0, 31, bit_step, t0)
    need = k_f - count(lambda key, ks: key > t)
    n_eq = count(lambda key, ks: key == t)

    def tie_search(_):
        def idx_step(i, x):
            cand = x | (1 << (30 - i))
            below = count(lambda key, ks: (key == t) & (lane + ks < cand))
            return jnp.where(below < need, cand, x)
        return lax.fori_loop(0, 31, idx_step, jnp.zeros((rows, 1), I32))

    x = lax.cond(jnp.max(n_eq - need) > 0.0, tie_search, lambda _: jnp.full((rows, 1), INT_MAX, I32), 0)

    def write(kb, carry):
        ks = pl.multiple_of(kb * width, width)
        key = _as_i32(sc_ref[:, pl.ds(ks, width)])
        sel = ((key > t) | ((key == t) & (lane + ks <= x))) & (key > KEY_NEG_INF)
        sc_ref[:, pl.ds(ks, width)] = jnp.where(sel, 0.0, NEG_INF)
        return carry

    lax.fori_loop(0, n_blocks, write, 0)


DSA_TK = 2 * LANES
DSA_FAR_TK = 2 * LANES


def _dsa_prompt_kernel(q_ref, qi_ref, wi_ref, kit_ref, kt_ref, v_ref, bias_ref, o_ref,
                       sc_ref, qh_ref, qih_ref, wib_ref, m_ref, l_ref, acc_ref, *, topk):
    blk = pl.program_id(1)
    tq = q_ref.shape[0]
    tk = DSA_TK
    qs = blk * tq
    n_wide = (blk + 2) // 2
    row = lax.broadcasted_iota(I32, (tq, tk), 0)
    lane = lax.broadcasted_iota(I32, (tq, tk), 1)

    low_half = lax.broadcasted_iota(I32, (tq, LANES), 1) < IDX_DIM
    for j in range(IDX_HEADS // 2):
        qp = qi_ref[:, j * LANES:(j + 1) * LANES]
        zero = jnp.zeros_like(qp)
        qih_ref[2 * j] = jnp.where(low_half, qp, zero)
        qih_ref[2 * j + 1] = jnp.where(low_half, zero, qp)
    wi = wi_ref[...] * IDX_SCALE
    for h in range(IDX_HEADS):
        wib_ref[h] = jnp.broadcast_to(wi[:, h:h + 1], (tq, tk))

    def score_block(kb, carry):
        ks = pl.multiple_of(kb * tk, tk)
        kit = kit_ref[:, pl.ds(ks, tk)]
        sc = jnp.zeros((tq, tk), F32)
        for h in range(IDX_HEADS):
            sc = sc + wib_ref[h] * jnp.maximum(_dot(qih_ref[h], kit), 0.0)
        sc = jnp.where(lane + ks <= row + qs, sc, NEG_INF)
        sc_ref[:, pl.ds(ks, tk)] = _as_f32(_ordered_key(sc))
        return carry

    lax.fori_loop(0, n_wide, score_block, 0)

    _topk_to_mask(sc_ref, n_wide, tk, tq, topk)

    _split_heads(q_ref, qh_ref)
    _flash_init(m_ref, l_ref, acc_ref)

    def block(ks, width, bias_off):
        mask = sc_ref[:, pl.ds(ks, width)]
        for h in range(ATT_HEADS):
            j = h // 2
            s = _dot(qh_ref[h], kt_ref[j * LANES:(j + 1) * LANES, pl.ds(ks, width)]) + mask
            if bias_off is not None:
                s = s + bias_ref[h, :, bias_off:bias_off + width]
            _flash_update(h, s, v_ref[pl.ds(ks, width), j * LANES:(j + 1) * LANES], m_ref, l_ref, acc_ref, True)

    n_far = jnp.maximum(blk - 1, 0)

    per_far = DSA_FAR_TK // LANES

    def far(kb, carry):
        block(pl.multiple_of(kb * DSA_FAR_TK, DSA_FAR_TK), DSA_FAR_TK, None)
        return carry

    def far_rest(kb, carry):
        block(pl.multiple_of(kb * LANES, LANES), LANES, None)
        return carry

    lax.fori_loop(0, n_far // per_far, far, 0)
    lax.fori_loop((n_far // per_far) * per_far, n_far, far_rest, 0)

    @pl.when(blk >= 1)
    def _():
        block(pl.multiple_of((blk - 1) * LANES, LANES), 2 * LANES, 0)

    @pl.when(blk == 0)
    def _():
        block(0, LANES, LANES)

    _flash_finish(o_ref, l_ref, acc_ref)


def _t5_bucket_table(max_dist):
    n = np.arange(max_dist + 1)
    nf = np.maximum(n, 1).astype(np.float32)
    scale = np.float32((T5_BUCKETS - T5_MAX_EXACT) / math.log(T5_MAX_DISTANCE / T5_MAX_EXACT))
    large = T5_MAX_EXACT + (np.log(nf / np.float32(T5_MAX_EXACT)) * scale).astype(np.int32)
    large = np.minimum(large, T5_BUCKETS - 1)
    return np.where(n < T5_MAX_EXACT, n, large).astype(np.int32)


def dsa_prompt_attend(q, qi, wi, kit2, kt, v, rel_bias, bsz, seq):
    tq = LANES
    hd = ATT_HEADS * ATT_DH
    nq = seq // tq
    topk = min(DSA_TOPK, seq // 4)
    buckets = _t5_bucket_table(seq)
    assert np.all(buckets[tq:] == buckets[tq])
    dist = np.arange(tq)[:, None] + tq - np.arange(2 * tq)[None, :]
    near = rel_bias[buckets[np.maximum(dist, 0)]] - rel_bias[buckets[tq]]
    near = jnp.transpose(near, (2, 0, 1)).astype(F32)
    once = pl.Buffered(1)
    return pl.pallas_call(
        functools.partial(_dsa_prompt_kernel, topk=topk),
        grid=(bsz, nq),
        in_specs=[pl.BlockSpec((tq, hd), lambda b_, i: (b_ * nq + i, 0)),
                  pl.BlockSpec((tq, IDX_HEADS * IDX_DIM), lambda b_, i: (b_ * nq + i, 0)),
                  pl.BlockSpec((tq, IDX_HEADS), lambda b_, i: (b_ * nq + i, 0)),
                  pl.BlockSpec((2 * IDX_DIM, seq), lambda b_, i: (0, b_), pipeline_mode=once),
                  pl.BlockSpec((hd, seq), lambda b_, i: (0, b_), pipeline_mode=once),
                  pl.BlockSpec((seq, hd), lambda b_, i: (b_, 0), pipeline_mode=once),
                  pl.BlockSpec((ATT_HEADS, tq, 2 * tq), lambda b_, i: (0, 0, 0), pipeline_mode=once)],
        out_specs=pl.BlockSpec((tq, hd), lambda b_, i: (b_ * nq + i, 0)),
        out_shape=jax.ShapeDtypeStruct((bsz * seq, hd), BF16),
        scratch_shapes=[pltpu.VMEM((tq, seq), F32), pltpu.VMEM((ATT_HEADS, tq, LANES), BF16),
                        pltpu.VMEM((IDX_HEADS, tq, LANES), BF16), pltpu.VMEM((IDX_HEADS, tq, DSA_TK), F32),
                        pltpu.VMEM((ATT_HEADS, tq, LANES), F32), pltpu.VMEM((ATT_HEADS, tq, LANES), F32),
                        pltpu.VMEM((ATT_HEADS, tq, LANES), F32)],
        compiler_params=_cparams(2, 56),
        name="dsa_prompt",
    )(q, qi, wi, kit2, kt, v, near)


def _dsa_sample_score_kernel(pt_ref, qi_ref, wi_ref, kin_ref, *refs):
    pp = PAGES_PER_STEP
    ki_refs = refs[:pp]
    sc_ref, scn_ref = refs[pp:]
    qi = qi_ref[0]
    wi = wi_ref[0] * IDX_SCALE
    for p_i in range(pp):
        d = jnp.maximum(_dot(qi, ki_refs[p_i][0].astype(BF16)), 0.0)
        sc_ref[0, :, p_i * PAGE_SIZE:(p_i + 1) * PAGE_SIZE] = jnp.sum(wi * d, axis=0, keepdims=True)
    d_new = jnp.sum(qi.astype(F32) * kin_ref[0].astype(BF16).astype(F32), axis=-1, keepdims=True)
    s_new = jnp.sum(wi * jnp.maximum(d_new, 0.0), axis=0, keepdims=True)
    scn_ref[0] = jnp.broadcast_to(s_new, (1, LANES))


def dsa_sample_scores(page_table, qi, wi, ki_new, pool_ki):
    pp = PAGES_PER_STEP
    bsz = qi.shape[0]
    n_pages = page_table.shape[1]
    past = n_pages * PAGE_SIZE
    pool_kit = jnp.swapaxes(pool_ki, 1, 2)
    page = lambda p_i: pl.BlockSpec((1, IDX_DIM, PAGE_SIZE), lambda b_, j, pt, p_i=p_i: (pt[b_, j * pp + p_i], 0, 0))
    sc, scn = pl.pallas_call(
        _dsa_sample_score_kernel,
        grid_spec=pltpu.PrefetchScalarGridSpec(
            num_scalar_prefetch=1, grid=(bsz, n_pages // pp),
            in_specs=[pl.BlockSpec((1, IDX_HEADS, IDX_DIM), lambda b_, j, pt: (b_, 0, 0)),
                      pl.BlockSpec((1, IDX_HEADS, 1), lambda b_, j, pt: (b_, 0, 0)),
                      pl.BlockSpec((1, 1, IDX_DIM), lambda b_, j, pt: (b_, 0, 0))] + [page(p_i) for p_i in range(pp)],
            out_specs=[pl.BlockSpec((1, 1, pp * PAGE_SIZE), lambda b_, j, pt: (b_, 0, j)),
                       pl.BlockSpec((1, 1, LANES), lambda b_, j, pt: (b_, 0, 0))]),
        out_shape=[jax.ShapeDtypeStruct((bsz, 1, past), F32), jax.ShapeDtypeStruct((bsz, 1, LANES), F32)],
        compiler_params=_cparams(2, 16),
        name="dsa_sample_scores",
    )(page_table, qi.reshape(bsz, IDX_HEADS, IDX_DIM), wi.reshape(bsz, IDX_HEADS, 1),
      ki_new.reshape(bsz, 1, IDX_DIM), *([pool_kit] * pp))
    pad = jnp.full((bsz, LANES - 1), NEG_INF, F32)
    return jnp.concatenate([sc[:, 0], scn[:, 0, :1], pad], axis=1)


def _dsa_sample_select_kernel(sc_ref, mask_ref, *, topk):
    rows, width = sc_ref.shape
    mask_ref[...] = _as_f32(_ordered_key(sc_ref[...]))
    _topk_to_mask(mask_ref, width // LANES, LANES, rows, topk)


def dsa_sample_select(scores, topk):
    rows, width = scores.shape
    return pl.pallas_call(
        functools.partial(_dsa_sample_select_kernel, topk=topk),
        grid=(1,),
        in_specs=[pl.BlockSpec((rows, width), lambda i: (0, 0))],
        out_specs=pl.BlockSpec((rows, width), lambda i: (0, 0)),
        out_shape=jax.ShapeDtypeStruct((rows, width), F32),
        compiler_params=_cparams(1, 16),
        name="dsa_sample_select",
    )(scores)


def dsa_layer(xp, xs, g, w_in, w_out, rel_bias, pool_k, pool_v, pool_ki, page_table, bsz, seq):
    hd = DSA_HEADS * DSA_DH
    e_qi, e_ki = 3 * hd + IDX_HEADS * IDX_DIM, 3 * hd + IDX_HEADS * IDX_DIM + IDX_DIM
    wq = w_in[:, :hd] * (DSA_DH ** -0.5)
    wk, wv = w_in[:, hd:2 * hd], w_in[:, 2 * hd:3 * hd]
    wqi, wki, wwi = w_in[:, 3 * hd:e_qi], w_in[:, e_qi:e_ki], w_in[:, e_ki:]
    wki2 = jnp.concatenate([wki, wki], axis=1)
    q, k, kt, v, vb, qi, ki, kit2, wi = norm_proj(
        xp, g, [(wq, False, [BF16]), (wk, False, [F32]), (wk, True, [BF16]), (wv, False, [F32, BF16]),
                (wqi, False, [BF16]), (wki, False, [F32]), (wki2, True, [BF16]), (wwi, False, [F32])], tm=512)
    o = dsa_prompt_attend(q, qi, wi, kit2, kt, vb, rel_bias, bsz, seq)
    yp = out_proj(o, w_out, xp, tm=512)

    dbs = xs.shape[0]
    qs, ks, vs, qis, kis, wis = norm_proj(
        xs, g, [(wq, False, [BF16]), (wk, False, [F32]), (wv, False, [F32]), (wqi, False, [BF16]),
                (wki, False, [F32]), (wwi, False, [F32])], tm=dbs)
    past = page_table.shape[1] * PAGE_SIZE
    scores = dsa_sample_scores(page_table, qis, wis, kis, pool_ki)
    mask = dsa_sample_select(scores, min(DSA_TOPK, (past + 1) // 4))
    buckets = _t5_bucket_table(past)
    dist = np.maximum(past - np.arange(past + LANES), 0)
    bias = rel_bias[buckets[dist]].T.astype(F32)
    os_ = paged_attend("dsa", page_table, qs, ks, vs, pool_k, pool_v, (mask.reshape(dbs, 1, past + LANES), bias))
    ys = out_proj(os_, w_out, xs, tm=dbs)
    shp = lambda t, n: t.reshape(n, -1, DSA_HEADS, DSA_DH)
    return (yp, ys, (shp(k, bsz), shp(v, bsz), ki.reshape(bsz, seq, IDX_DIM)),
            (shp(ks, dbs), shp(vs, dbs), kis.reshape(dbs, 1, IDX_DIM)))


def _logf_rows_kernel(f_ref, b_ref, o_ref):
    o_ref[...] = _log_sigmoid(f_ref[...] + b_ref[...])


def fox_logf_rows(f, b_f):
    n, h = f.shape
    return pl.pallas_call(
        _logf_rows_kernel,
        grid=(1,),
        in_specs=[pl.BlockSpec((n, h), lambda i: (0, 0)), pl.BlockSpec((1, h), lambda i: (0, 0))],
        out_specs=pl.BlockSpec((n, h), lambda i: (0, 0)),
        out_shape=jax.ShapeDtypeStruct((n, h), F32),
        name="fox_logf_rows",
    )(f, b_f.reshape(1, h))


def kernel(x_prompt, x_sample, state_mlstm_C, state_mlstm_n, state_mlstm_m, cache_dsa_k, cache_dsa_v,
           cache_dsa_kidx, state_rglru_h, state_rglru_conv, cache_fox_k, cache_fox_v, cache_fox_logf, page_table,
           norm_mix, norm_mlp, norm_final, mlstm_w_in, mlstm_b_gate, mlstm_norm, mlstm_w_out, dsa_w_in, dsa_w_out,
           rel_bias, rglru_w_in, rglru_conv_w, rglru_conv_b, rglru_w_a, rglru_b_a, rglru_w_x, rglru_b_x, rglru_lam,
           rglru_w_out, fox_w_in, fox_b_f, fox_w_out, mlp_w1, mlp_w2):
    bsz, seq, d = x_prompt.shape
    dbs = x_sample.shape[0]
    assert x_sample.shape[1] == 1
    depth = norm_mix.shape[0]
    yp = x_prompt.reshape(bsz * seq, d)
    ys = x_sample.reshape(dbs, d)
    states = {kind: ([], []) for kind in range(4)}
    for layer in range(depth):
        kind, inst = layer % 4, layer // 4
        g = norm_mix[layer]
        if kind == 0:
            yp, ys, st_p, st_s = mlstm_layer(yp, ys, g, mlstm_w_in[inst], mlstm_b_gate[inst], mlstm_norm[inst],
                                             mlstm_w_out[inst], state_mlstm_C[inst], state_mlstm_n[inst],
                                             state_mlstm_m[inst], bsz, seq)
        elif kind == 1:
            yp, ys, st_p, st_s = dsa_layer(yp, ys, g, dsa_w_in[inst], dsa_w_out[inst], rel_bias, cache_dsa_k[inst],
                                           cache_dsa_v[inst], cache_dsa_kidx[inst], page_table, bsz, seq)
        elif kind == 2:
            yp, ys, st_p, st_s = rglru_layer(yp, ys, g, rglru_w_in[inst], rglru_conv_w[inst], rglru_conv_b[inst],
                                             rglru_w_a[inst], rglru_b_a[inst], rglru_w_x[inst], rglru_b_x[inst],
                                             rglru_lam[inst], rglru_w_out[inst], state_rglru_h[inst],
                                             state_rglru_conv[inst], bsz, seq)
        else:
            yp, ys, st_p, st_s = fox_layer(yp, ys, g, fox_w_in[inst], fox_b_f[inst], fox_w_out[inst],
                                           cache_fox_k[inst], cache_fox_v[inst], cache_fox_logf[inst], page_table,
                                           bsz, seq)
        states[kind][0].append(st_p)
        states[kind][1].append(st_s)
        last = layer == depth - 1
        yp = mlp(yp, norm_mlp[layer], mlp_w1[layer], mlp_w2[layer], norm_final, last, tm=1024, tf=512)
        ys = mlp(ys, norm_mlp[layer], mlp_w1[layer], mlp_w2[layer], norm_final, last, tm=dbs, tf=512)

    def stk(kind, group, j):
        return jnp.stack([entry[j] for entry in states[kind][group]])

    outs = [yp.reshape(bsz, seq, d), ys.reshape(dbs, 1, d)]
    for kind, n_leaves in ((0, 3), (1, 3), (2, 2), (3, 3)):
        for group in (0, 1):
            outs += [stk(kind, group, j) for j in range(n_leaves)]
    return tuple(outs)
```

```python
import functools
import math

import numpy as np
import jax
import jax.numpy as jnp
from jax import lax
from jax.experimental import pallas as pl
from jax.experimental.pallas import tpu as pltpu

F32 = jnp.float32
BF16 = jnp.bfloat16
I32 = jnp.int32
NEG_INF = float("-inf")
F32_LOWEST = float(np.finfo(np.float32).min)
LOG2E = math.log2(math.e)

RMS_EPS = 1e-6
D_MODEL = 1024
PAGE_SIZE = 128

MLSTM_HEADS = 8
MLSTM_DQK = 64
MLSTM_DV = 128
MLSTM_CHUNK = 128

DSA_HEADS = 16
DSA_DH = 64
IDX_HEADS = 8
IDX_DIM = 64
DSA_TOPK = 256
IDX_SCALE = (IDX_DIM * IDX_HEADS) ** -0.5
T5_BUCKETS = 32
T5_MAX_EXACT = 16
T5_MAX_DISTANCE = 128

RG_WIDTH = 1280
RG_BLOCKS = 10
RG_BLOCK = 128
RG_CONV = 4
RG_C = 8.0

FOX_HEADS = 16
FOX_DH = 64
FOX_TQ = 256

MLP_TM = 1024
MLP_TF = 512

VMEM_BUDGET_BYTES = 56 * 1024 * 1024
KEY_NEG_INF = np.int32(-2139095041)
INT_MIN = np.int32(-2147483648)
INT_MAX = np.int32(2147483647)


def _cparams(n_grid, vmem_mb=32):
    return pltpu.CompilerParams(dimension_semantics=("arbitrary",) * n_grid,
                                vmem_limit_bytes=min(vmem_mb * 1024 * 1024, VMEM_BUDGET_BYTES))


def _log_sigmoid(x):
    return jnp.minimum(x, 0.0) - jnp.log1p(jnp.exp(-jnp.abs(x)))


def _rms(x, g):
    return x * lax.rsqrt(jnp.mean(x * x, axis=-1, keepdims=True) + RMS_EPS) * g


def _dot(a, b):
    return jnp.dot(a, b, preferred_element_type=F32)


def _dot_t(a, b):
    return lax.dot_general(a, b, (((1,), (1,)), ((), ())), preferred_element_type=F32)


def _norm_proj_kernel(x_ref, g_ref, *refs, plan):
    n_w = len(plan)
    w_refs = refs[:n_w]
    o_refs = refs[n_w:]
    xn = _rms(x_ref[...], g_ref[...]).astype(BF16)
    oi = 0
    for w_ref, (transposed, dtypes) in zip(w_refs, plan):
        o = _dot_t(w_ref[...], xn) if transposed else _dot(xn, w_ref[...])
        for dt in dtypes:
            o_refs[oi][...] = o.astype(dt)
            oi += 1


def norm_proj(x, g, segments, tm, bsz=1):
    n_rows, d = x.shape
    seq = n_rows // bsz
    assert seq % tm == 0
    per_seq = seq // tm
    plan, w_args, w_specs, out_shapes, out_specs = [], [], [], [], []
    for w, transposed, dtypes in segments:
        n = w.shape[1]
        wb = w.astype(BF16)
        if transposed:
            wb = wb.T
            w_specs.append(pl.BlockSpec((n, d), lambda i: (0, 0)))
        else:
            w_specs.append(pl.BlockSpec((d, n), lambda i: (0, 0)))
        w_args.append(wb)
        plan.append((transposed, tuple(dtypes)))
        for dt in dtypes:
            if transposed:
                out_shapes.append(jax.ShapeDtypeStruct((bsz, n, seq), dt))
                out_specs.append(pl.BlockSpec((None, n, tm), lambda i: (i // per_seq, 0, i % per_seq)))
            else:
                out_shapes.append(jax.ShapeDtypeStruct((n_rows, n), dt))
                out_specs.append(pl.BlockSpec((tm, n), lambda i: (i, 0)))
    return pl.pallas_call(
        functools.partial(_norm_proj_kernel, plan=tuple(plan)),
        grid=(n_rows // tm,),
        in_specs=[pl.BlockSpec((tm, d), lambda i: (i, 0)), pl.BlockSpec((1, d), lambda i: (0, 0))] + w_specs,
        out_specs=out_specs,
        out_shape=out_shapes,
        compiler_params=_cparams(1, 48),
        name="norm_proj",
    )(x, g.reshape(1, d), *w_args)


def _out_proj_kernel(a_ref, w_ref, r_ref, o_ref):
    o_ref[...] = r_ref[...] + _dot(a_ref[...], w_ref[...])


def out_proj(a, w, resid, tm):
    n_rows, k = a.shape
    d = w.shape[1]
    return pl.pallas_call(
        _out_proj_kernel,
        grid=(n_rows // tm,),
        in_specs=[pl.BlockSpec((tm, k), lambda i: (i, 0)), pl.BlockSpec((k, d), lambda i: (0, 0)),
                  pl.BlockSpec((tm, d), lambda i: (i, 0))],
        out_specs=pl.BlockSpec((tm, d), lambda i: (i, 0)),
        out_shape=jax.ShapeDtypeStruct((n_rows, d), F32),
        compiler_params=_cparams(1, 32),
        name="out_proj",
    )(a, w.astype(BF16), resid)


def _mlp_kernel(x_ref, g_ref, w1_ref, w2_ref, gf_ref, o_ref, xn_ref, acc_ref, *, final_norm):
    f = pl.program_id(1)

    @pl.when(f == 0)
    def _():
        xn_ref[...] = _rms(x_ref[...], g_ref[...]).astype(BF16)
        acc_ref[...] = jnp.zeros_like(acc_ref)

    h = jnp.maximum(_dot(xn_ref[...], w1_ref[...]), 0.0)
    acc_ref[...] += _dot((h * h).astype(BF16), w2_ref[...])

    @pl.when(f == pl.num_programs(1) - 1)
    def _():
        y = x_ref[...] + acc_ref[...]
        o_ref[...] = _rms(y, gf_ref[...]) if final_norm else y


def mlp(x, g, w1, w2, g_final, final_norm, tm, tf):
    n_rows, d = x.shape
    d_ff = w1.shape[1]
    return pl.pallas_call(
        functools.partial(_mlp_kernel, final_norm=final_norm),
        grid=(n_rows // tm, d_ff // tf),
        in_specs=[pl.BlockSpec((tm, d), lambda i, f: (i, 0)), pl.BlockSpec((1, d), lambda i, f: (0, 0)),
                  pl.BlockSpec((d, tf), lambda i, f: (0, f)), pl.BlockSpec((tf, d), lambda i, f: (f, 0)),
                  pl.BlockSpec((1, d), lambda i, f: (0, 0))],
        out_specs=pl.BlockSpec((tm, d), lambda i, f: (i, 0)),
        out_shape=jax.ShapeDtypeStruct((n_rows, d), F32),
        scratch_shapes=[pltpu.VMEM((tm, d), BF16), pltpu.VMEM((tm, d), F32)],
        compiler_params=_cparams(2, 48),
        name="mlp",
    )(x, g.reshape(1, d), w1.astype(BF16), w2.astype(BF16), g_final.reshape(1, d))


def _lane_cumsum(x):
    lane = lax.broadcasted_iota(I32, x.shape, 1)
    d = 1
    while d < x.shape[1]:
        x = x + jnp.where(lane >= d, pltpu.roll(x, d, axis=1), 0.0)
        d *= 2
    return x


def _lane_cummax(x):
    lane = lax.broadcasted_iota(I32, x.shape, 1)
    d = 1
    while d < x.shape[1]:
        x = jnp.maximum(x, jnp.where(lane >= d, pltpu.roll(x, d, axis=1), NEG_INF))
        d *= 2
    return x


def _mlstm_prompt_kernel(q_ref, kt_ref, v_ref, o_ref, gt_ref, bg_ref, ng_ref,
                         hh_ref, cst_ref, mst_ref, ct_ref, m_ref):
    c = pl.program_id(1)
    L = q_ref.shape[0]
    H, DQK, DV = MLSTM_HEADS, MLSTM_DQK, MLSTM_DV

    @pl.when(c == 0)
    def _():
        ct_ref[...] = jnp.zeros_like(ct_ref)
        m_ref[...] = jnp.full_like(m_ref, NEG_INF)

    gates = gt_ref[...] + bg_ref[...]
    ig = gates[:H]
    lf = _log_sigmoid(gates[H:])
    b = _lane_cumsum(lf)
    a = ig - b
    m0 = m_ref[...]
    big_m = jnp.maximum(m0, _lane_cummax(a))
    m = b + big_m
    cols = jnp.concatenate([big_m, m, jnp.zeros((128 - 2 * H, L), F32)], axis=0).T
    row_i = lax.broadcasted_iota(I32, (L, L), 0)
    col_i = lax.broadcasted_iota(I32, (L, L), 1)
    causal = col_i <= row_i
    ones_col = (lax.broadcasted_iota(I32, (L, DV), 1) == 0).astype(BF16)
    for h in range(H):
        q = q_ref[:, h * DQK:(h + 1) * DQK]
        kt = kt_ref[h * DQK:(h + 1) * DQK, :]
        v_ext = jnp.concatenate([v_ref[:, h * DV:(h + 1) * DV], ones_col], axis=1)
        big_m_col = cols[:, h:h + 1]
        m_col = cols[:, H + h:H + h + 1]
        w = jnp.exp(jnp.where(causal, a[h:h + 1, :] - big_m_col, NEG_INF))
        w_inter = jnp.exp(m0[h:h + 1, :] - big_m_col)
        s = _dot(q, kt) * w
        ct = ct_ref[h]
        tot = w_inter * _dot(q, ct.astype(BF16)) + _dot(s.astype(BF16), v_ext)
        num = tot[:, :DV]
        den = tot[:, DV:DV + 1]
        hv = num / jnp.maximum(jnp.abs(den), jnp.exp(-m_col))
        hv = _rms(hv, ng_ref[h:h + 1, :])
        gate = jax.nn.sigmoid(o_ref[:, h * DV:(h + 1) * DV])
        hh_ref[:, h * DV:(h + 1) * DV] = (hv * gate).astype(hh_ref.dtype)
        g_row = w[L - 1:L, :]
        decay = w_inter[L - 1:L, :]
        ct_ref[h] = decay * ct + _dot((kt.astype(F32) * g_row).astype(BF16), v_ext)
    m_ref[...] = m[:, L - 1:L]

    @pl.when(c == pl.num_programs(1) - 1)
    def _():
        cst_ref[0] = ct_ref[...]
        mst_ref[0] = m_ref[...]


def mlstm_prompt(q, kt, v, o, gt, b_gate, norm_g, bsz, seq):
    H, DQK, DV, L = MLSTM_HEADS, MLSTM_DQK, MLSTM_DV, MLSTM_CHUNK
    nc = seq // L
    n_rows = bsz * seq
    row = lambda b_, c_: (b_ * nc + c_, 0)
    col = lambda b_, c_: (b_, 0, c_)
    fixed = lambda b_, c_: (0, 0)
    hh, cst, mst = pl.pallas_call(
        _mlstm_prompt_kernel,
        grid=(bsz, nc),
        in_specs=[pl.BlockSpec((L, H * DQK), row), pl.BlockSpec((None, H * DQK, L), col),
                  pl.BlockSpec((L, H * DV), row), pl.BlockSpec((L, H * DV), row),
                  pl.BlockSpec((None, 2 * H, L), col), pl.BlockSpec((2 * H, 1), fixed),
                  pl.BlockSpec((H, DV), fixed)],
        out_specs=[pl.BlockSpec((L, H * DV), row),
                   pl.BlockSpec((1, H, DQK, 2 * DV), lambda b_, c_: (b_, 0, 0, 0)),
                   pl.BlockSpec((1, H, 1), lambda b_, c_: (b_, 0, 0))],
        out_shape=[jax.ShapeDtypeStruct((n_rows, H * DV), BF16),
                   jax.ShapeDtypeStruct((bsz, H, DQK, 2 * DV), F32),
                   jax.ShapeDtypeStruct((bsz, H, 1), F32)],
        scratch_shapes=[pltpu.VMEM((H, DQK, 2 * DV), F32), pltpu.VMEM((H, 1), F32)],
        compiler_params=_cparams(2, 32),
        name="mlstm_prompt",
    )(q, kt, v, o, gt, b_gate.reshape(2 * H, 1), norm_g)
    c_new = jnp.swapaxes(cst[..., :DV], 2, 3)
    n_new = cst[..., DV]
    return hh, c_new, n_new, mst[..., 0]


def _mlstm_sample_kernel(q_ref, k_ref, v_ref, vt_ref, o_ref, g_ref, bg_ref, ng_ref, c0_ref, n0_ref, m0_ref,
                         hh_ref, c_ref, n_ref, m_ref):
    H = MLSTM_HEADS
    gates = g_ref[0] + bg_ref[...]
    ig = gates[:H]
    lf = _log_sigmoid(gates[H:])
    m0 = m0_ref[0]
    m = jnp.maximum(lf + m0, ig)
    w_inter = jnp.exp(lf + m0 - m)
    w_intra = jnp.exp(ig - m)
    qb = q_ref[0]
    q32 = qb.astype(F32)
    k = k_ref[0]
    v = v_ref[0]
    s = jnp.sum(q32 * k.astype(BF16).astype(F32), axis=-1, keepdims=True) * w_intra
    n0 = n0_ref[0]
    den = w_inter * jnp.sum(q32 * n0.astype(BF16).astype(F32), axis=-1, keepdims=True) + s
    row_i = lax.broadcasted_iota(I32, (H, MLSTM_DV), 0)
    inter = jnp.zeros((H, MLSTM_DV), F32)
    for h in range(H):
        c0 = c0_ref[0, h]
        inter = jnp.where(row_i == h, _dot_t(qb, c0.astype(BF16)), inter)
        c_ref[0, h] = w_inter[h:h + 1, :] * c0 + (w_intra[h:h + 1, :] * vt_ref[0, :, h:h + 1]) * k[h:h + 1, :]
    num = w_inter * inter + s * v.astype(BF16).astype(F32)
    hv = num / jnp.maximum(jnp.abs(den), jnp.exp(-m))
    hv = _rms(hv, ng_ref[...])
    hh_ref[0] = (hv * jax.nn.sigmoid(o_ref[0])).astype(hh_ref.dtype)
    n_ref[0] = w_inter * n0 + w_intra * k
    m_ref[0] = m


def mlstm_sample(q, k, v, o, gates, b_gate, norm_g, c0, n0, m0):
    H, DQK, DV = MLSTM_HEADS, MLSTM_DQK, MLSTM_DV
    bsz = q.shape[0]
    q3 = q.reshape(bsz, H, DQK)
    k3 = k.reshape(bsz, H, DQK)
    v3 = v.reshape(bsz, H, DV)
    vt = jnp.swapaxes(v3, 1, 2)
    b3 = lambda *blk: pl.BlockSpec((1,) + blk, lambda i: (i,) + (0,) * len(blk))
    fixed = lambda *blk: pl.BlockSpec(blk, lambda i: (0,) * len(blk))
    hh, c_new, n_new, m_new = pl.pallas_call(
        _mlstm_sample_kernel,
        grid=(bsz,),
        in_specs=[b3(H, DQK), b3(H, DQK), b3(H, DV), b3(DV, H), b3(H, DV), b3(2 * H, 1), fixed(2 * H, 1),
                  fixed(H, DV), b3(H, DV, DQK), b3(H, DQK), b3(H, 1)],
        out_specs=[b3(H, DV), b3(H, DV, DQK), b3(H, DQK), b3(H, 1)],
        out_shape=[jax.ShapeDtypeStruct((bsz, H, DV), BF16), jax.ShapeDtypeStruct((bsz, H, DV, DQK), F32),
                   jax.ShapeDtypeStruct((bsz, H, DQK), F32), jax.ShapeDtypeStruct((bsz, H, 1), F32)],
        compiler_params=_cparams(1, 16),
        name="mlstm_sample",
    )(q3, k3, v3, vt, o.reshape(bsz, H, DV), gates.reshape(bsz, 2 * H, 1), b_gate.reshape(2 * H, 1), norm_g,
      c0, n0, m0.reshape(bsz, H, 1))
    return hh.reshape(bsz, H * DV), c_new, n_new, m_new[..., 0]


def mlstm_layer(xp, xs, g, w_in, b_gate, norm_g, w_out, c0, n0, m0, bsz, seq):
    H, DQK, DV = MLSTM_HEADS, MLSTM_DQK, MLSTM_DV
    e_q, e_k, e_v, e_o = H * DQK, 2 * H * DQK, 2 * H * DQK + H * DV, 2 * H * DQK + 2 * H * DV
    wq = w_in[:, :e_q] * (DQK ** -0.5)
    wk, wv, wo, wg = w_in[:, e_q:e_k], w_in[:, e_k:e_v], w_in[:, e_v:e_o], w_in[:, e_o:]
    q, kt, v, o, gt = norm_proj(xp, g, [(wq, False, [BF16]), (wk, True, [BF16]), (wv, False, [BF16]),
                                        (wo, False, [F32]), (wg, True, [F32])], tm=512, bsz=bsz)
    hh, c_p, n_p, m_p = mlstm_prompt(q, kt, v, o, gt, b_gate, norm_g, bsz, seq)
    yp = out_proj(hh, w_out, xp, tm=512)
    qs, ks, vs, os_, gs = norm_proj(xs, g, [(wq, False, [BF16]), (wk, False, [F32]), (wv, False, [F32]),
                                            (wo, False, [F32]), (wg, False, [F32])], tm=xs.shape[0])
    hhs, c_s, n_s, m_s = mlstm_sample(qs, ks, vs, os_, gs, b_gate, norm_g, c0, n0, m0)
    ys = out_proj(hhs, w_out, xs, tm=xs.shape[0])
    return yp, ys, (c_p, n_p, m_p), (c_s, n_s, m_s)


def _softplus(x):
    return jnp.maximum(x, 0.0) + jnp.log1p(jnp.exp(-jnp.abs(x)))


def _expm1(x):
    u = jnp.exp(x)
    um1 = u - 1.0
    return jnp.where(um1 == 0.0, x, jnp.where(um1 == -1.0, -1.0, um1 * x / jnp.log(u)))


def _rglru_gates(conv, n, wa_ref, wx_ref, ba_ref, bx_ref, lam_ref):
    sl = slice(n * RG_BLOCK, (n + 1) * RG_BLOCK)
    cb = conv.astype(BF16)
    r = jax.nn.sigmoid(_dot(cb, wa_ref[n]) + ba_ref[:, sl])
    i = jax.nn.sigmoid(_dot(cb, wx_ref[n]) + bx_ref[:, sl])
    log_a = -RG_C * r * _softplus(-lam_ref[:, sl])
    a = jnp.exp(log_a)
    u = jnp.sqrt(-_expm1(2.0 * log_a)) * (i * conv)
    return a, u


def _rglru_prompt_kernel(gate_ref, xb_ref, cw_ref, cb_ref, wa_ref, wx_ref, ba_ref, bx_ref, lam_ref,
                         y_ref, h_ref, tail_ref, xpad_ref, hc_ref):
    c = pl.program_id(1)
    T = xb_ref.shape[0]

    @pl.when(c == 0)
    def _():
        xpad_ref[0:8, :] = jnp.zeros((8, RG_WIDTH), F32)
        hc_ref[...] = jnp.zeros_like(hc_ref)

    xpad_ref[8:8 + T, :] = xb_ref[...]
    row = lax.broadcasted_iota(I32, (T, RG_BLOCK), 0)
    for n in range(RG_BLOCKS):
        sl = slice(n * RG_BLOCK, (n + 1) * RG_BLOCK)
        conv = cb_ref[:, sl]
        for j in range(RG_CONV):
            conv = conv + cw_ref[j:j + 1, sl] * xpad_ref[8 - (RG_CONV - 1) + j:8 - (RG_CONV - 1) + j + T, sl]
        a, u = _rglru_gates(conv, n, wa_ref, wx_ref, ba_ref, bx_ref, lam_ref)
        d = 1
        while d < T:
            keep = row >= d
            a_sh = jnp.where(keep, pltpu.roll(a, d, axis=0), 1.0)
            u_sh = jnp.where(keep, pltpu.roll(u, d, axis=0), 0.0)
            u = a * u_sh + u
            a = a * a_sh
            d *= 2
        h = a * hc_ref[:, sl] + u
        hc_ref[:, sl] = h[T - 1:T, :]
        y_ref[:, sl] = (h * jax.nn.gelu(gate_ref[:, sl])).astype(y_ref.dtype)
    xpad_ref[0:8, :] = xb_ref[T - 8:T, :]

    @pl.when(c == pl.num_programs(1) - 1)
    def _():
        h_ref[0] = hc_ref[...]
        tail_ref[0] = xb_ref[T - 8:T, :]


def rglru_prompt(gate, xb, conv_w, conv_b, w_a, b_a, w_x, b_x, lam, bsz, seq, tt):
    W = RG_WIDTH
    nc = seq // tt
    row = lambda b_, c_: (b_ * nc + c_, 0)
    fixed2 = lambda b_, c_: (0, 0)
    fixed3 = lambda b_, c_: (0, 0, 0)
    y, h, tail = pl.pallas_call(
        _rglru_prompt_kernel,
        grid=(bsz, nc),
        in_specs=[pl.BlockSpec((tt, W), row), pl.BlockSpec((tt, W), row),
                  pl.BlockSpec((RG_CONV, W), fixed2), pl.BlockSpec((1, W), fixed2),
                  pl.BlockSpec((RG_BLOCKS, RG_BLOCK, RG_BLOCK), fixed3),
                  pl.BlockSpec((RG_BLOCKS, RG_BLOCK, RG_BLOCK), fixed3),
                  pl.BlockSpec((1, W), fixed2), pl.BlockSpec((1, W), fixed2), pl.BlockSpec((1, W), fixed2)],
        out_specs=[pl.BlockSpec((tt, W), row), pl.BlockSpec((1, 1, W), lambda b_, c_: (b_, 0, 0)),
                   pl.BlockSpec((1, 8, W), lambda b_, c_: (b_, 0, 0))],
        out_shape=[jax.ShapeDtypeStruct((bsz * seq, W), BF16), jax.ShapeDtypeStruct((bsz, 1, W), F32),
                   jax.ShapeDtypeStruct((bsz, 8, W), F32)],
        scratch_shapes=[pltpu.VMEM((tt + 8, W), F32), pltpu.VMEM((1, W), F32)],
        compiler_params=_cparams(2, 32),
        name="rglru_prompt",
    )(gate, xb, conv_w, conv_b.reshape(1, W), w_a.astype(BF16), w_x.astype(BF16),
      b_a.reshape(1, W), b_x.reshape(1, W), lam.reshape(1, W))
    return y, h[:, 0], tail[:, 8 - (RG_CONV - 1):]


def _rglru_sample_kernel(gate_ref, xb_ref, c0_ref, c1_ref, c2_ref, h0_ref, cw_ref, cb_ref, wa_ref, wx_ref,
                         ba_ref, bx_ref, lam_ref, y_ref, h_ref):
    taps = (c0_ref, c1_ref, c2_ref, xb_ref)
    for n in range(RG_BLOCKS):
        sl = slice(n * RG_BLOCK, (n + 1) * RG_BLOCK)
        conv = cb_ref[:, sl]
        for j in range(RG_CONV):
            conv = conv + cw_ref[j:j + 1, sl] * taps[j][:, sl]
        a, u = _rglru_gates(conv, n, wa_ref, wx_ref, ba_ref, bx_ref, lam_ref)
        h = a * h0_ref[:, sl] + u
        h_ref[:, sl] = h
        y_ref[:, sl] = (h * jax.nn.gelu(gate_ref[:, sl])).astype(y_ref.dtype)


def rglru_sample(gate, xb, conv0, h0, conv_w, conv_b, w_a, b_a, w_x, b_x, lam):
    W = RG_WIDTH
    bsz = gate.shape[0]
    full2 = lambda r: pl.BlockSpec((r, W), lambda i: (0, 0))
    full3 = pl.BlockSpec((RG_BLOCKS, RG_BLOCK, RG_BLOCK), lambda i: (0, 0, 0))
    y, h = pl.pallas_call(
        _rglru_sample_kernel,
        grid=(1,),
        in_specs=[full2(bsz)] * 6 + [full2(RG_CONV), full2(1), full3, full3, full2(1), full2(1), full2(1)],
        out_specs=[full2(bsz), full2(bsz)],
        out_shape=[jax.ShapeDtypeStruct((bsz, W), BF16), jax.ShapeDtypeStruct((bsz, W), F32)],
        compiler_params=_cparams(1, 16),
        name="rglru_sample",
    )(gate, xb, conv0[:, 0], conv0[:, 1], conv0[:, 2], h0, conv_w, conv_b.reshape(1, W),
      w_a.astype(BF16), w_x.astype(BF16), b_a.reshape(1, W), b_x.reshape(1, W), lam.reshape(1, W))
    new_conv = jnp.concatenate([conv0[:, 1:], xb[:, None, :]], axis=1)
    return y, h, new_conv


def rglru_layer(xp, xs, g, w_in, conv_w, conv_b, w_a, b_a, w_x, b_x, lam, w_out, h0, conv0, bsz, seq):
    W = RG_WIDTH
    segs = [(w_in[:, :W], False, [F32]), (w_in[:, W:], False, [F32])]
    gate, xb = norm_proj(xp, g, segs, tm=512)
    y, h_p, conv_p = rglru_prompt(gate, xb, conv_w, conv_b, w_a, b_a, w_x, b_x, lam, bsz, seq, tt=256)
    yp = out_proj(y, w_out, xp, tm=512)
    gate_s, xb_s = norm_proj(xs, g, segs, tm=xs.shape[0])
    y_s, h_s, conv_s = rglru_sample(gate_s, xb_s, conv0, h0, conv_w, conv_b, w_a, b_a, w_x, b_x, lam)
    ys = out_proj(y_s, w_out, xs, tm=xs.shape[0])
    return yp, ys, (h_p, conv_p), (h_s, conv_s)


ATT_HEADS = 16
ATT_DH = 64
ATT_PAIRS = ATT_HEADS // 2
LANES = 128


def _split_heads(q_ref, qh_ref):
    tq = q_ref.shape[0]
    low_half = lax.broadcasted_iota(I32, (tq, LANES), 1) < ATT_DH
    for j in range(ATT_PAIRS):
        qp = q_ref[:, j * LANES:(j + 1) * LANES]
        zero = jnp.zeros_like(qp)
        qh_ref[2 * j] = jnp.where(low_half, qp, zero)
        qh_ref[2 * j + 1] = jnp.where(low_half, zero, qp)


def _lane_fold(x, op):
    out = x[:, :LANES]
    for c in range(1, x.shape[1] // LANES):
        out = op(out, x[:, c * LANES:(c + 1) * LANES])
    return out


def _flash_update(h, s, v_pair, m_ref, l_ref, acc_ref):
    reps = s.shape[1] // LANES
    m_old = m_ref[h]
    m_new = jnp.maximum(m_old, jnp.max(_lane_fold(s, jnp.maximum), axis=-1, keepdims=True))
    m_ref[h] = m_new
    alpha = jnp.exp2(m_old - m_new)
    p = jnp.exp2(s - jnp.concatenate([m_new] * reps, axis=1))
    l_ref[h] = alpha * l_ref[h] + _lane_fold(p, jnp.add)
    acc_ref[h] = alpha * acc_ref[h] + _dot(p.astype(BF16), v_pair)


def _flash_finish(o_ref, l_ref, acc_ref):
    tq = o_ref.shape[0]
    low_half = lax.broadcasted_iota(I32, (tq, LANES), 1) < ATT_DH
    for j in range(ATT_PAIRS):
        lo = acc_ref[2 * j] / jnp.sum(l_ref[2 * j], axis=-1, keepdims=True)
        hi = acc_ref[2 * j + 1] / jnp.sum(l_ref[2 * j + 1], axis=-1, keepdims=True)
        o_ref[:, j * LANES:(j + 1) * LANES] = jnp.where(low_half, lo, hi).astype(o_ref.dtype)


def _flash_init(m_ref, l_ref, acc_ref):
    m_ref[...] = jnp.full_like(m_ref, F32_LOWEST)
    l_ref[...] = jnp.zeros_like(l_ref)
    acc_ref[...] = jnp.zeros_like(acc_ref)


def _fox_gate_kernel(ft_ref, bft_ref, ct_ref, logf_ref, carry_ref):
    c = pl.program_id(1)
    ts = ft_ref.shape[1]

    @pl.when(c == 0)
    def _():
        carry_ref[...] = jnp.zeros_like(carry_ref)

    logf = _log_sigmoid(ft_ref[...] + bft_ref[...])
    logf_ref[...] = logf
    cs = _lane_cumsum(logf) + carry_ref[...]
    ct_ref[...] = cs
    carry_ref[...] = cs[:, ts - 1:ts]


def fox_gates(ft, b_f, ts):
    bsz, H, seq = ft.shape
    blk = pl.BlockSpec((None, H, ts), lambda b_, c_: (b_, 0, c_))
    return pl.pallas_call(
        _fox_gate_kernel,
        grid=(bsz, seq // ts),
        in_specs=[blk, pl.BlockSpec((H, 1), lambda b_, c_: (0, 0))],
        out_specs=[blk, blk],
        out_shape=[jax.ShapeDtypeStruct(ft.shape, F32), jax.ShapeDtypeStruct(ft.shape, F32)],
        scratch_shapes=[pltpu.VMEM((H, 1), F32)],
        compiler_params=_cparams(2, 16),
        name="fox_gates",
    )(ft, b_f.reshape(H, 1))


def _fox_prompt_kernel(q_ref, kt_ref, v_ref, ct_ref, o_ref, qh_ref, m_ref, l_ref, acc_ref):
    qi = pl.program_id(1)
    tq = q_ref.shape[0]
    qs = pl.multiple_of(qi * tq, tq)
    _split_heads(q_ref, qh_ref)
    _flash_init(m_ref, l_ref, acc_ref)
    c_q0 = ct_ref[:, pl.ds(qs, LANES)][:, 0:1]
    causal = (lax.broadcasted_iota(I32, (tq, tq), 1) <= lax.broadcasted_iota(I32, (tq, tq), 0))

    def block(kb, diagonal):
        ks = pl.multiple_of(kb * tq, tq)
        decay = (c_q0 - ct_ref[:, pl.ds(ks, tq)]) * LOG2E
        for h in range(ATT_HEADS):
            j = h // 2
            s = _dot(qh_ref[h], kt_ref[j * LANES:(j + 1) * LANES, pl.ds(ks, tq)]) + decay[h:h + 1, :]
            if diagonal:
                s = jnp.where(causal, s, NEG_INF)
            _flash_update(h, s, v_ref[pl.ds(ks, tq), j * LANES:(j + 1) * LANES], m_ref, l_ref, acc_ref)

    def far(kb, carry):
        block(kb, False)
        return carry

    lax.fori_loop(0, qi, far, 0)
    block(qi, True)
    _flash_finish(o_ref, l_ref, acc_ref)


def fox_prompt_attend(q, kt, v, ct, bsz, seq, tq):
    hd = ATT_HEADS * ATT_DH
    nq = seq // tq
    once = pl.Buffered(1)
    return pl.pallas_call(
        _fox_prompt_kernel,
        grid=(bsz, nq),
        in_specs=[pl.BlockSpec((tq, hd), lambda b_, i: (b_ * nq + i, 0)),
                  pl.BlockSpec((None, hd, seq), lambda b_, i: (b_, 0, 0), pipeline_mode=once),
                  pl.BlockSpec((seq, hd), lambda b_, i: (b_, 0), pipeline_mode=once),
                  pl.BlockSpec((None, ATT_HEADS, seq), lambda b_, i: (b_, 0, 0), pipeline_mode=once)],
        out_specs=pl.BlockSpec((tq, hd), lambda b_, i: (b_ * nq + i, 0)),
        out_shape=jax.ShapeDtypeStruct((bsz * seq, hd), BF16),
        scratch_shapes=[pltpu.VMEM((ATT_HEADS, tq, LANES), BF16), pltpu.VMEM((ATT_HEADS, tq, LANES), F32),
                        pltpu.VMEM((ATT_HEADS, tq, LANES), F32), pltpu.VMEM((ATT_HEADS, tq, LANES), F32)],
        compiler_params=_cparams(2, 56),
        name="fox_prompt",
    )(q, kt, v, ct)


PAGES_PER_STEP = 8


def _paged_attend_kernel(pt_ref, qb_ref, qr_ref, kn_ref, vn_ref, *refs, mode):
    pp = PAGES_PER_STEP
    k_refs, v_refs = refs[:pp], refs[pp:2 * pp]
    refs = refs[2 * pp:]
    if mode == "fox":
        lf_refs, lfn_ref = refs[:pp], refs[pp]
        o_ref, m_ref, l_ref, acc_ref, cc_ref = refs[pp + 1:]
    else:
        mask_ref, maskn_ref, bias_ref, biasn_ref = refs[:4]
        o_ref, m_ref, l_ref, acc_ref = refs[4:]
    j = pl.program_id(1)
    H = ATT_HEADS

    @pl.when(j == 0)
    def _():
        _flash_init(m_ref, l_ref, acc_ref)
        if mode == "fox":
            cc_ref[...] = jnp.zeros_like(cc_ref)

    head_row = lax.broadcasted_iota(I32, (H, PAGE_SIZE), 0)

    def softmax_step(s):
        m_old = m_ref[...]
        m_new = jnp.maximum(m_old, jnp.max(s, axis=-1, keepdims=True))
        m_ref[...] = m_new
        alpha = jnp.exp(m_old - m_new)
        p = jnp.exp(s - m_new)
        l_ref[...] = alpha * l_ref[...] + jnp.sum(p, axis=-1, keepdims=True)
        return alpha, p

    logits = []
    cc = cc_ref[...] if mode == "fox" else None
    for p_i in range(pp):
        s = jnp.zeros((H, PAGE_SIZE), F32)
        for h in range(H):
            part = jnp.sum(k_refs[p_i][0, h] * qb_ref[0, h], axis=0, keepdims=True)
            s = jnp.where(head_row == h, part, s)
        if mode == "fox":
            c = _lane_cumsum(lf_refs[p_i][0]) + cc
            cc = c[:, PAGE_SIZE - 1:PAGE_SIZE]
            s = s - c
        else:
            sl = slice(p_i * PAGE_SIZE, (p_i + 1) * PAGE_SIZE)
            s = s + mask_ref[0][:, sl] + bias_ref[:, sl]
        logits.append(s)
    if mode == "fox":
        cc_ref[...] = cc
    alpha, p = softmax_step(jnp.concatenate(logits, axis=1))
    for h in range(H):
        a = acc_ref[h] * alpha[h:h + 1, :]
        for p_i in range(pp):
            a = a + v_refs[p_i][0, h] * p[h:h + 1, p_i * PAGE_SIZE:(p_i + 1) * PAGE_SIZE]
        acc_ref[h] = a

    @pl.when(j == pl.num_programs(1) - 1)
    def _():
        s = jnp.sum(qr_ref[0] * kn_ref[0], axis=-1, keepdims=True)
        if mode == "fox":
            s = s - (cc_ref[...] + lfn_ref[0])
        else:
            s = s + maskn_ref[0][:, 0:1] + biasn_ref[:, 0:1]
        alpha, p = softmax_step(s)
        out_row = lax.broadcasted_iota(I32, (H, LANES), 0)
        out = jnp.zeros((H, LANES), F32)
        pad = jnp.zeros((LANES - ATT_DH, PAGE_SIZE), F32)
        for h in range(H):
            a_t = jnp.concatenate([acc_ref[h] * alpha[h:h + 1, :], pad], axis=0).T
            out = jnp.where(out_row == h, jnp.sum(a_t, axis=0, keepdims=True), out)
        out = (out[:, :ATT_DH] + p * vn_ref[0]) / l_ref[...]
        o_ref[0] = out.astype(o_ref.dtype)


def paged_attend(mode, page_table, q, k_new, v_new, pool_k, pool_v, extras):
    pp = PAGES_PER_STEP
    bsz, hd = q.shape
    H, DH = ATT_HEADS, ATT_DH
    n_pages = page_table.shape[1]
    past = n_pages * PAGE_SIZE
    assert n_pages % pp == 0
    pool_kt = jnp.transpose(pool_k, (0, 2, 3, 1))
    pool_vt = jnp.transpose(pool_v, (0, 2, 3, 1))
    q3 = q.astype(F32).reshape(bsz, H, DH)
    q_lanes = jnp.broadcast_to(q3[..., None], (bsz, H, DH, PAGE_SIZE))
    row3 = pl.BlockSpec((1, H, DH), lambda b_, j, pt: (b_, 0, 0))
    page = lambda p_i, *blk: pl.BlockSpec((1,) + blk, lambda b_, j, pt, p_i=p_i: (pt[b_, j * pp + p_i],) + (0,) * len(blk))
    in_specs = [pl.BlockSpec((1, H, DH, PAGE_SIZE), lambda b_, j, pt: (b_, 0, 0, 0)), row3, row3, row3]
    in_specs += [page(p_i, H, DH, PAGE_SIZE) for p_i in range(pp)] * 2
    args = [q_lanes, q3, k_new.reshape(bsz, H, DH), v_new.reshape(bsz, H, DH)]
    args += [pool_kt] * pp + [pool_vt] * pp
    scratch = [pltpu.VMEM((H, 1), F32), pltpu.VMEM((H, 1), F32), pltpu.VMEM((H, DH, PAGE_SIZE), F32)]
    if mode == "fox":
        pool_lft, lf_new = extras
        in_specs += [page(p_i, ATT_HEADS, PAGE_SIZE) for p_i in range(pp)]
        in_specs += [pl.BlockSpec((1, ATT_HEADS, 1), lambda b_, j, pt: (b_, 0, 0))]
        args += [pool_lft] * pp + [lf_new]
        scratch += [pltpu.VMEM((ATT_HEADS, 1), F32)]
    else:
        mask, bias = extras
        in_specs += [pl.BlockSpec((1, 1, pp * PAGE_SIZE), lambda b_, j, pt: (b_, 0, j)),
                     pl.BlockSpec((1, 1, LANES), lambda b_, j, pt: (b_, 0, past // LANES)),
                     pl.BlockSpec((ATT_HEADS, pp * PAGE_SIZE), lambda b_, j, pt: (0, j)),
                     pl.BlockSpec((ATT_HEADS, LANES), lambda b_, j, pt: (0, past // LANES))]
        args += [mask, mask, bias, bias]
    out = pl.pallas_call(
        functools.partial(_paged_attend_kernel, mode=mode),
        grid_spec=pltpu.PrefetchScalarGridSpec(
            num_scalar_prefetch=1, grid=(bsz, n_pages // pp), in_specs=in_specs,
            out_specs=pl.BlockSpec((1, H, DH), lambda b_, j, pt: (b_, 0, 0)), scratch_shapes=scratch),
        out_shape=jax.ShapeDtypeStruct((bsz, H, DH), BF16),
        compiler_params=_cparams(2, 48),
        name="paged_attend_" + mode,
    )(page_table, *args)
    return out.reshape(bsz, hd)


def fox_layer(xp, xs, g, w_in, b_f, w_out, pool_k, pool_v, pool_logf, page_table, bsz, seq):
    hd = FOX_HEADS * FOX_DH
    wq = w_in[:, :hd] * (FOX_DH ** -0.5)
    wk, wv, wf = w_in[:, hd:2 * hd], w_in[:, 2 * hd:3 * hd], w_in[:, 3 * hd:]
    q, ktf, kt, vb, vtf, ft = norm_proj(xp, g, [(wq * LOG2E, False, [BF16]), (wk, True, [F32, BF16]),
                                                (wv, False, [BF16]), (wv, True, [F32]), (wf, True, [F32])],
                                        tm=512, bsz=bsz)
    ct, logft = fox_gates(ft, b_f, ts=min(seq, 2048))
    o = fox_prompt_attend(q, kt, vb, ct, bsz, seq, tq=FOX_TQ)
    yp = out_proj(o, w_out, xp, tm=512)
    k = jnp.transpose(ktf.reshape(bsz, FOX_HEADS, FOX_DH, seq), (0, 3, 1, 2))
    v = jnp.transpose(vtf.reshape(bsz, FOX_HEADS, FOX_DH, seq), (0, 3, 1, 2))
    logf = jnp.swapaxes(logft, 1, 2)
    dbs = xs.shape[0]
    qs, ks, vs, fs = norm_proj(xs, g, [(wq, False, [BF16]), (wk, False, [F32]), (wv, False, [F32]),
                                       (wf, False, [F32])], tm=dbs)
    lf_s = fox_logf_rows(fs, b_f)
    os_ = paged_attend("fox", page_table, qs, ks, vs, pool_k, pool_v,
                       (jnp.swapaxes(pool_logf, 1, 2), lf_s.reshape(dbs, FOX_HEADS, 1)))
    ys = out_proj(os_, w_out, xs, tm=dbs)
    shp = lambda t, n: t.reshape(n, -1, FOX_HEADS, FOX_DH)
    return (yp, ys, (k, v, logf), (shp(ks, dbs), shp(vs, dbs), lf_s.reshape(dbs, 1, FOX_HEADS)))


def _ordered_key(x):
    b = lax.bitcast_convert_type(x + 0.0, I32)
    return b ^ ((b >> 31) & INT_MAX)


def _as_f32(x):
    return lax.bitcast_convert_type(x, F32)


def _as_i32(x):
    return lax.bitcast_convert_type(x, I32)


def _topk_to_mask(sc_ref, n_blocks, width, rows, topk):
    lane = lax.broadcasted_iota(I32, (rows, width), 1)
    k_f = float(topk)

    def count(pred):
        def body(kb, acc):
            ks = pl.multiple_of(kb * width, width)
            hit = jnp.where(pred(_as_i32(sc_ref[:, pl.ds(ks, width)]), ks), 1.0, 0.0)
            return acc + _lane_fold(hit, jnp.add)
        acc = lax.fori_loop(0, n_blocks, body, jnp.zeros((rows, LANES), F32))
        return jnp.sum(acc, axis=-1, keepdims=True)

    t0 = jnp.where(count(lambda key, ks: key >= 0) >= k_f, 0, INT_MIN).astype(I32)

    def bit_step(i, t):
        cand = t | (1 << (30 - i))
        return jnp.where(count(lambda key, ks: key >= cand) >= k_f, cand, t)

    t = lax.fori_loop(0, 31, bit_step, t0)
    need = k_f - count(lambda key, ks: key > t)
    n_eq = count(lambda key, ks: key == t)

    def tie_search(_):
        def idx_step(i, x):
            cand = x | (1 << (30 - i))
            below = count(lambda key, ks: (key == t) & (lane + ks < cand))
            return jnp.where(below < need, cand, x)
        return lax.fori_loop(0, 31, idx_step, jnp.zeros((rows, 1), I32))

    x = lax.cond(jnp.max(n_eq - need) > 0.0, tie_search, lambda _: jnp.full((rows, 1), INT_MAX, I32), 0)

    def write(kb, carry):
        ks = pl.multiple_of(kb * width, width)
        key = _as_i32(sc_ref[:, pl.ds(ks, width)])
        sel = ((key > t) | ((key == t) & (lane + ks <= x))) & (key > KEY_NEG_INF)
        sc_ref[:, pl.ds(ks, width)] = jnp.where(sel, 0.0, NEG_INF)
        return carry

    lax.fori_loop(0, n_blocks, write, 0)


DSA_TK = 2 * LANES
DSA_FAR_TK = 2 * LANES


def _dsa_prompt_kernel(q_ref, qi_ref, wi_ref, kit_ref, kt_ref, v_ref, bias_ref, o_ref,
                       sc_ref, qh_ref, qih_ref, wib_ref, m_ref, l_ref, acc_ref, *, topk):
    blk = pl.program_id(1)
    tq = q_ref.shape[0]
    tk = DSA_TK
    qs = blk * tq
    n_wide = (blk + 2) // 2
    row = lax.broadcasted_iota(I32, (tq, tk), 0)
    lane = lax.broadcasted_iota(I32, (tq, tk), 1)

    low_half = lax.broadcasted_iota(I32, (tq, LANES), 1) < IDX_DIM
    for j in range(IDX_HEADS // 2):
        qp = qi_ref[:, j * LANES:(j + 1) * LANES]
        zero = jnp.zeros_like(qp)
        qih_ref[2 * j] = jnp.where(low_half, qp, zero)
        qih_ref[2 * j + 1] = jnp.where(low_half, zero, qp)
    wi = wi_ref[...] * IDX_SCALE
    for h in range(IDX_HEADS):
        wib_ref[h] = jnp.broadcast_to(wi[:, h:h + 1], (tq, tk))

    def score_block(kb, carry):
        ks = pl.multiple_of(kb * tk, tk)
        kit = kit_ref[:, pl.ds(ks, tk)]
        sc = jnp.zeros((tq, tk), F32)
        for h in range(IDX_HEADS):
            sc = sc + wib_ref[h] * jnp.maximum(_dot(qih_ref[h], kit), 0.0)
        sc = jnp.where(lane + ks <= row + qs, sc, NEG_INF)
        sc_ref[:, pl.ds(ks, tk)] = _as_f32(_ordered_key(sc))
        return carry

    lax.fori_loop(0, n_wide, score_block, 0)

    _topk_to_mask(sc_ref, n_wide, tk, tq, topk)

    _split_heads(q_ref, qh_ref)
    _flash_init(m_ref, l_ref, acc_ref)

    def block(ks, width, bias_off):
        mask = sc_ref[:, pl.ds(ks, width)]
        for h in range(ATT_HEADS):
            j = h // 2
            s = _dot(qh_ref[h], kt_ref[j * LANES:(j + 1) * LANES, pl.ds(ks, width)]) + mask
            if bias_off is not None:
                s = s + bias_ref[h, :, bias_off:bias_off + width]
            _flash_update(h, s, v_ref[pl.ds(ks, width), j * LANES:(j + 1) * LANES], m_ref, l_ref, acc_ref)

    n_far = jnp.maximum(blk - 1, 0)

    per_far = DSA_FAR_TK // LANES

    def far(kb, carry):
        block(pl.multiple_of(kb * DSA_FAR_TK, DSA_FAR_TK), DSA_FAR_TK, None)
        return carry

    def far_rest(kb, carry):
        block(pl.multiple_of(kb * LANES, LANES), LANES, None)
        return carry

    lax.fori_loop(0, n_far // per_far, far, 0)
    lax.fori_loop((n_far // per_far) * per_far, n_far, far_rest, 0)

    @pl.when(blk >= 1)
    def _():
        block(pl.multiple_of((blk - 1) * LANES, LANES), 2 * LANES, 0)

    @pl.when(blk == 0)
    def _():
        block(0, LANES, LANES)

    _flash_finish(o_ref, l_ref, acc_ref)


def _t5_bucket_table(max_dist):
    n = np.arange(max_dist + 1)
    nf = np.maximum(n, 1).astype(np.float32)
    scale = np.float32((T5_BUCKETS - T5_MAX_EXACT) / math.log(T5_MAX_DISTANCE / T5_MAX_EXACT))
    large = T5_MAX_EXACT + (np.log(nf / np.float32(T5_MAX_EXACT)) * scale).astype(np.int32)
    large = np.minimum(large, T5_BUCKETS - 1)
    return np.where(n < T5_MAX_EXACT, n, large).astype(np.int32)


def dsa_prompt_attend(q, qi, wi, kit2, kt, v, rel_bias, bsz, seq):
    tq = LANES
    hd = ATT_HEADS * ATT_DH
    nq = seq // tq
    topk = min(DSA_TOPK, seq // 4)
    buckets = _t5_bucket_table(seq)
    assert np.all(buckets[tq:] == buckets[tq])
    dist = np.arange(tq)[:, None] + tq - np.arange(2 * tq)[None, :]
    one_hot = jax.nn.one_hot(buckets[np.maximum(dist, 0)].reshape(-1), T5_BUCKETS, dtype=F32)
    near = jnp.dot(one_hot, rel_bias, precision=lax.Precision.HIGHEST) - rel_bias[buckets[tq]]
    near = jnp.transpose(near.reshape(tq, 2 * tq, ATT_HEADS) * LOG2E, (2, 0, 1)).astype(F32)
    once = pl.Buffered(1)
    return pl.pallas_call(
        functools.partial(_dsa_prompt_kernel, topk=topk),
        grid=(bsz, nq),
        in_specs=[pl.BlockSpec((tq, hd), lambda b_, i: (b_ * nq + i, 0)),
                  pl.BlockSpec((tq, IDX_HEADS * IDX_DIM), lambda b_, i: (b_ * nq + i, 0)),
                  pl.BlockSpec((tq, IDX_HEADS), lambda b_, i: (b_ * nq + i, 0)),
                  pl.BlockSpec((None, 2 * IDX_DIM, seq), lambda b_, i: (b_, 0, 0), pipeline_mode=once),
                  pl.BlockSpec((None, hd, seq), lambda b_, i: (b_, 0, 0), pipeline_mode=once),
                  pl.BlockSpec((seq, hd), lambda b_, i: (b_, 0), pipeline_mode=once),
                  pl.BlockSpec((ATT_HEADS, tq, 2 * tq), lambda b_, i: (0, 0, 0), pipeline_mode=once)],
        out_specs=pl.BlockSpec((tq, hd), lambda b_, i: (b_ * nq + i, 0)),
        out_shape=jax.ShapeDtypeStruct((bsz * seq, hd), BF16),
        scratch_shapes=[pltpu.VMEM((tq, seq), F32), pltpu.VMEM((ATT_HEADS, tq, LANES), BF16),
                        pltpu.VMEM((IDX_HEADS, tq, LANES), BF16), pltpu.VMEM((IDX_HEADS, tq, DSA_TK), F32),
                        pltpu.VMEM((ATT_HEADS, tq, LANES), F32), pltpu.VMEM((ATT_HEADS, tq, LANES), F32),
                        pltpu.VMEM((ATT_HEADS, tq, LANES), F32)],
        compiler_params=_cparams(2, 56),
        name="dsa_prompt",
    )(q, qi, wi, kit2, kt, v, near)


def _dsa_sample_score_kernel(pt_ref, qi_ref, wi_ref, kin_ref, *refs):
    pp = PAGES_PER_STEP
    ki_refs = refs[:pp]
    sc_ref, scn_ref = refs[pp:]
    qi = qi_ref[0]
    wi = wi_ref[0] * IDX_SCALE
    for p_i in range(pp):
        d = jnp.maximum(_dot(qi, ki_refs[p_i][0].astype(BF16)), 0.0)
        sc_ref[0, :, p_i * PAGE_SIZE:(p_i + 1) * PAGE_SIZE] = jnp.sum(wi * d, axis=0, keepdims=True)
    d_new = jnp.sum(qi.astype(F32) * kin_ref[0].astype(BF16).astype(F32), axis=-1, keepdims=True)
    s_new = jnp.sum(wi * jnp.maximum(d_new, 0.0), axis=0, keepdims=True)
    scn_ref[0] = jnp.broadcast_to(s_new, (1, LANES))


def dsa_sample_scores(page_table, qi, wi, ki_new, pool_ki):
    pp = PAGES_PER_STEP
    bsz = qi.shape[0]
    n_pages = page_table.shape[1]
    past = n_pages * PAGE_SIZE
    pool_kit = jnp.swapaxes(pool_ki, 1, 2)
    page = lambda p_i: pl.BlockSpec((1, IDX_DIM, PAGE_SIZE), lambda b_, j, pt, p_i=p_i: (pt[b_, j * pp + p_i], 0, 0))
    sc, scn = pl.pallas_call(
        _dsa_sample_score_kernel,
        grid_spec=pltpu.PrefetchScalarGridSpec(
            num_scalar_prefetch=1, grid=(bsz, n_pages // pp),
            in_specs=[pl.BlockSpec((1, IDX_HEADS, IDX_DIM), lambda b_, j, pt: (b_, 0, 0)),
                      pl.BlockSpec((1, IDX_HEADS, 1), lambda b_, j, pt: (b_, 0, 0)),
                      pl.BlockSpec((1, 1, IDX_DIM), lambda b_, j, pt: (b_, 0, 0))] + [page(p_i) for p_i in range(pp)],
            out_specs=[pl.BlockSpec((1, 1, pp * PAGE_SIZE), lambda b_, j, pt: (b_, 0, j)),
                       pl.BlockSpec((1, 1, LANES), lambda b_, j, pt: (b_, 0, 0))]),
        out_shape=[jax.ShapeDtypeStruct((bsz, 1, past), F32), jax.ShapeDtypeStruct((bsz, 1, LANES), F32)],
        compiler_params=_cparams(2, 16),
        name="dsa_sample_scores",
    )(page_table, qi.reshape(bsz, IDX_HEADS, IDX_DIM), wi.reshape(bsz, IDX_HEADS, 1),
      ki_new.reshape(bsz, 1, IDX_DIM), *([pool_kit] * pp))
    pad = jnp.full((bsz, LANES - 1), NEG_INF, F32)
    return jnp.concatenate([sc[:, 0], scn[:, 0, :1], pad], axis=1)


def _dsa_sample_select_kernel(sc_ref, mask_ref, *, topk):
    rows, width = sc_ref.shape
    mask_ref[...] = _as_f32(_ordered_key(sc_ref[...]))
    _topk_to_mask(mask_ref, width // LANES, LANES, rows, topk)


def dsa_sample_select(scores, topk):
    rows, width = scores.shape
    return pl.pallas_call(
        functools.partial(_dsa_sample_select_kernel, topk=topk),
        grid=(1,),
        in_specs=[pl.BlockSpec((rows, width), lambda i: (0, 0))],
        out_specs=pl.BlockSpec((rows, width), lambda i: (0, 0)),
        out_shape=jax.ShapeDtypeStruct((rows, width), F32),
        compiler_params=_cparams(1, 16),
        name="dsa_sample_select",
    )(scores)


def dsa_layer(xp, xs, g, w_in, w_out, rel_bias, pool_k, pool_v, pool_ki, page_table, bsz, seq):
    hd = DSA_HEADS * DSA_DH
    e_qi, e_ki = 3 * hd + IDX_HEADS * IDX_DIM, 3 * hd + IDX_HEADS * IDX_DIM + IDX_DIM
    wq = w_in[:, :hd] * (DSA_DH ** -0.5)
    wk, wv = w_in[:, hd:2 * hd], w_in[:, 2 * hd:3 * hd]
    wqi, wki, wwi = w_in[:, 3 * hd:e_qi], w_in[:, e_qi:e_ki], w_in[:, e_ki:]
    wki2 = jnp.concatenate([wki, wki], axis=1)
    q, ktf, kt, vb, vtf, qi, kit2f, kit2, wi = norm_proj(
        xp, g, [(wq * LOG2E, False, [BF16]), (wk, True, [F32, BF16]), (wv, False, [BF16]), (wv, True, [F32]),
                (wqi, False, [BF16]), (wki2, True, [F32, BF16]), (wwi, False, [F32])], tm=512, bsz=bsz)
    o = dsa_prompt_attend(q, qi, wi, kit2, kt, vb, rel_bias, bsz, seq)
    yp = out_proj(o, w_out, xp, tm=512)
    k = jnp.transpose(ktf.reshape(bsz, DSA_HEADS, DSA_DH, seq), (0, 3, 1, 2))
    v = jnp.transpose(vtf.reshape(bsz, DSA_HEADS, DSA_DH, seq), (0, 3, 1, 2))
    ki = jnp.swapaxes(kit2f[:, :IDX_DIM, :], 1, 2)

    dbs = xs.shape[0]
    qs, ks, vs, qis, kis, wis = norm_proj(
        xs, g, [(wq, False, [BF16]), (wk, False, [F32]), (wv, False, [F32]), (wqi, False, [BF16]),
                (wki, False, [F32]), (wwi, False, [F32])], tm=dbs)
    past = page_table.shape[1] * PAGE_SIZE
    scores = dsa_sample_scores(page_table, qis, wis, kis, pool_ki)
    mask = dsa_sample_select(scores, min(DSA_TOPK, (past + 1) // 4))
    buckets = _t5_bucket_table(past)
    dist = np.maximum(past - np.arange(past + LANES), 0)
    bias = rel_bias[buckets[dist]].T.astype(F32)
    os_ = paged_attend("dsa", page_table, qs, ks, vs, pool_k, pool_v, (mask.reshape(dbs, 1, past + LANES), bias))
    ys = out_proj(os_, w_out, xs, tm=dbs)
    shp = lambda t, n: t.reshape(n, -1, DSA_HEADS, DSA_DH)
    return (yp, ys, (k, v, ki), (shp(ks, dbs), shp(vs, dbs), kis.reshape(dbs, 1, IDX_DIM)))


def _logf_rows_kernel(f_ref, b_ref, o_ref):
    o_ref[...] = _log_sigmoid(f_ref[...] + b_ref[...])


def fox_logf_rows(f, b_f):
    n, h = f.shape
    return pl.pallas_call(
        _logf_rows_kernel,
        grid=(1,),
        in_specs=[pl.BlockSpec((n, h), lambda i: (0, 0)), pl.BlockSpec((1, h), lambda i: (0, 0))],
        out_specs=pl.BlockSpec((n, h), lambda i: (0, 0)),
        out_shape=jax.ShapeDtypeStruct((n, h), F32),
        name="fox_logf_rows",
    )(f, b_f.reshape(1, h))


def kernel(x_prompt, x_sample, state_mlstm_C, state_mlstm_n, state_mlstm_m, cache_dsa_k, cache_dsa_v,
           cache_dsa_kidx, state_rglru_h, state_rglru_conv, cache_fox_k, cache_fox_v, cache_fox_logf, page_table,
           norm_mix, norm_mlp, norm_final, mlstm_w_in, mlstm_b_gate, mlstm_norm, mlstm_w_out, dsa_w_in, dsa_w_out,
           rel_bias, rglru_w_in, rglru_conv_w, rglru_conv_b, rglru_w_a, rglru_b_a, rglru_w_x, rglru_b_x, rglru_lam,
           rglru_w_out, fox_w_in, fox_b_f, fox_w_out, mlp_w1, mlp_w2):
    bsz, seq, d = x_prompt.shape
    dbs = x_sample.shape[0]
    assert x_sample.shape[1] == 1
    depth = norm_mix.shape[0]
    yp = x_prompt.reshape(bsz * seq, d)
    ys = x_sample.reshape(dbs, d)
    states = {kind: ([], []) for kind in range(4)}
    for layer in range(depth):
        kind, inst = layer % 4, layer // 4
        g = norm_mix[layer]
        if kind == 0:
            yp, ys, st_p, st_s = mlstm_layer(yp, ys, g, mlstm_w_in[inst], mlstm_b_gate[inst], mlstm_norm[inst],
                                             mlstm_w_out[inst], state_mlstm_C[inst], state_mlstm_n[inst],
                                             state_mlstm_m[inst], bsz, seq)
        elif kind == 1:
            yp, ys, st_p, st_s = dsa_layer(yp, ys, g, dsa_w_in[inst], dsa_w_out[inst], rel_bias, cache_dsa_k[inst],
                                           cache_dsa_v[inst], cache_dsa_kidx[inst], page_table, bsz, seq)
        elif kind == 2:
            yp, ys, st_p, st_s = rglru_layer(yp, ys, g, rglru_w_in[inst], rglru_conv_w[inst], rglru_conv_b[inst],
                                             rglru_w_a[inst], rglru_b_a[inst], rglru_w_x[inst], rglru_b_x[inst],
                                             rglru_lam[inst], rglru_w_out[inst], state_rglru_h[inst],
                                             state_rglru_conv[inst], bsz, seq)
        else:
            yp, ys, st_p, st_s = fox_layer(yp, ys, g, fox_w_in[inst], fox_b_f[inst], fox_w_out[inst],
                                           cache_fox_k[inst], cache_fox_v[inst], cache_fox_logf[inst], page_table,
                                           bsz, seq)
        states[kind][0].append(st_p)
        states[kind][1].append(st_s)
        last = layer == depth - 1
        yp = mlp(yp, norm_mlp[layer], mlp_w1[layer], mlp_w2[layer], norm_final, last, tm=1024, tf=512)
        ys = mlp(ys, norm_mlp[layer], mlp_w1[layer], mlp_w2[layer], norm_final, last, tm=dbs, tf=512)

    def stk(kind, group, j):
        return jnp.stack([entry[j] for entry in states[kind][group]])

    outs = [yp.reshape(bsz, seq, d), ys.reshape(dbs, 1, d)]
    for kind, n_leaves in ((0, 3), (1, 3), (2, 2), (3, 3)):
        for group in (0, 1):
            outs += [stk(kind, group, j) for j in range(n_leaves)]
    return tuple(outs)
```

```python
import functools
import math

import numpy as np
import jax
import jax.numpy as jnp
from jax import lax
from jax.experimental import pallas as pl
from jax.experimental.pallas import tpu as pltpu

F32 = jnp.float32
BF16 = jnp.bfloat16
I32 = jnp.int32
NEG_INF = float("-inf")
F32_LOWEST = float(np.finfo(np.float32).min)
LOG2E = math.log2(math.e)

RMS_EPS = 1e-6
D_MODEL = 1024
PAGE_SIZE = 128

MLSTM_HEADS = 8
MLSTM_DQK = 64
MLSTM_DV = 128
MLSTM_CHUNK = 128

DSA_HEADS = 16
DSA_DH = 64
IDX_HEADS = 8
IDX_DIM = 64
DSA_TOPK = 256
IDX_SCALE = (IDX_DIM * IDX_HEADS) ** -0.5
T5_BUCKETS = 32
T5_MAX_EXACT = 16
T5_MAX_DISTANCE = 128

RG_WIDTH = 1280
RG_BLOCKS = 10
RG_BLOCK = 128
RG_CONV = 4
RG_C = 8.0
RG_TT = 256

FOX_HEADS = 16
FOX_DH = 64
FOX_TQ = 256

MLP_TM = 1024
MLP_TF = 512

VMEM_BUDGET_BYTES = 56 * 1024 * 1024
KEY_NEG_INF = np.int32(-2139095041)
INT_MIN = np.int32(-2147483648)
INT_MAX = np.int32(2147483647)


def _cparams(n_grid, vmem_mb=32):
    return pltpu.CompilerParams(dimension_semantics=("arbitrary",) * n_grid,
                                vmem_limit_bytes=min(vmem_mb * 1024 * 1024, VMEM_BUDGET_BYTES))


def _log_sigmoid(x):
    return jnp.minimum(x, 0.0) - jnp.log1p(jnp.exp(-jnp.abs(x)))


def _rms(x, g):
    return x * lax.rsqrt(jnp.mean(x * x, axis=-1, keepdims=True) + RMS_EPS) * g


def _dot(a, b):
    return jnp.dot(a, b, preferred_element_type=F32)


def _dot_t(a, b):
    return lax.dot_general(a, b, (((1,), (1,)), ((), ())), preferred_element_type=F32)


def _norm_proj_kernel(x_ref, g_ref, *refs, plan):
    n_w = len(plan)
    w_refs = refs[:n_w]
    o_refs = refs[n_w:]
    xn = _rms(x_ref[...], g_ref[...]).astype(BF16)
    oi = 0
    for w_ref, (transposed, dtypes) in zip(w_refs, plan):
        o = _dot_t(w_ref[...], xn) if transposed else _dot(xn, w_ref[...])
        for dt in dtypes:
            o_refs[oi][...] = o.astype(dt)
            oi += 1


def norm_proj(x, g, segments, tm, bsz=1):
    n_rows, d = x.shape
    seq = n_rows // bsz
    assert seq % tm == 0
    per_seq = seq // tm
    plan, w_args, w_specs, out_shapes, out_specs = [], [], [], [], []
    for w, transposed, dtypes in segments:
        n = w.shape[1]
        wb = w.astype(BF16)
        if transposed:
            wb = wb.T
            w_specs.append(pl.BlockSpec((n, d), lambda i: (0, 0)))
        else:
            w_specs.append(pl.BlockSpec((d, n), lambda i: (0, 0)))
        w_args.append(wb)
        plan.append((transposed, tuple(dtypes)))
        for dt in dtypes:
            if transposed:
                out_shapes.append(jax.ShapeDtypeStruct((bsz, n, seq), dt))
                out_specs.append(pl.BlockSpec((None, n, tm), lambda i: (i // per_seq, 0, i % per_seq)))
            else:
                out_shapes.append(jax.ShapeDtypeStruct((n_rows, n), dt))
                out_specs.append(pl.BlockSpec((tm, n), lambda i: (i, 0)))
    return pl.pallas_call(
        functools.partial(_norm_proj_kernel, plan=tuple(plan)),
        grid=(n_rows // tm,),
        in_specs=[pl.BlockSpec((tm, d), lambda i: (i, 0)), pl.BlockSpec((1, d), lambda i: (0, 0))] + w_specs,
        out_specs=out_specs,
        out_shape=out_shapes,
        compiler_params=_cparams(1, 48),
        name="norm_proj",
    )(x, g.reshape(1, d), *w_args)


def _out_proj_kernel(a_ref, w_ref, r_ref, o_ref):
    o_ref[...] = r_ref[...] + _dot(a_ref[...], w_ref[...])


def out_proj(a, w, resid, tm):
    n_rows, k = a.shape
    d = w.shape[1]
    return pl.pallas_call(
        _out_proj_kernel,
        grid=(n_rows // tm,),
        in_specs=[pl.BlockSpec((tm, k), lambda i: (i, 0)), pl.BlockSpec((k, d), lambda i: (0, 0)),
                  pl.BlockSpec((tm, d), lambda i: (i, 0))],
        out_specs=pl.BlockSpec((tm, d), lambda i: (i, 0)),
        out_shape=jax.ShapeDtypeStruct((n_rows, d), F32),
        compiler_params=_cparams(1, 32),
        name="out_proj",
    )(a, w.astype(BF16), resid)


def _mlp_kernel(x_ref, g_ref, w1_ref, w2_ref, gf_ref, o_ref, xn_ref, acc_ref, *, final_norm):
    f = pl.program_id(1)

    @pl.when(f == 0)
    def _():
        xn_ref[...] = _rms(x_ref[...], g_ref[...]).astype(BF16)
        acc_ref[...] = jnp.zeros_like(acc_ref)

    h = jnp.maximum(_dot(xn_ref[...], w1_ref[...]), 0.0)
    acc_ref[...] += _dot((h * h).astype(BF16), w2_ref[...])

    @pl.when(f == pl.num_programs(1) - 1)
    def _():
        y = x_ref[...] + acc_ref[...]
        o_ref[...] = _rms(y, gf_ref[...]) if final_norm else y


def mlp(x, g, w1, w2, g_final, final_norm, tm, tf):
    n_rows, d = x.shape
    d_ff = w1.shape[1]
    return pl.pallas_call(
        functools.partial(_mlp_kernel, final_norm=final_norm),
        grid=(n_rows // tm, d_ff // tf),
        in_specs=[pl.BlockSpec((tm, d), lambda i, f: (i, 0)), pl.BlockSpec((1, d), lambda i, f: (0, 0)),
                  pl.BlockSpec((d, tf), lambda i, f: (0, f)), pl.BlockSpec((tf, d), lambda i, f: (f, 0)),
                  pl.BlockSpec((1, d), lambda i, f: (0, 0))],
        out_specs=pl.BlockSpec((tm, d), lambda i, f: (i, 0)),
        out_shape=jax.ShapeDtypeStruct((n_rows, d), F32),
        scratch_shapes=[pltpu.VMEM((tm, d), BF16), pltpu.VMEM((tm, d), F32)],
        compiler_params=_cparams(2, 48),
        name="mlp",
    )(x, g.reshape(1, d), w1.astype(BF16), w2.astype(BF16), g_final.reshape(1, d))


def _lane_cumsum(x):
    lane = lax.broadcasted_iota(I32, x.shape, 1)
    d = 1
    while d < x.shape[1]:
        x = x + jnp.where(lane >= d, pltpu.roll(x, d, axis=1), 0.0)
        d *= 2
    return x


def _lane_cummax(x):
    lane = lax.broadcasted_iota(I32, x.shape, 1)
    d = 1
    while d < x.shape[1]:
        x = jnp.maximum(x, jnp.where(lane >= d, pltpu.roll(x, d, axis=1), NEG_INF))
        d *= 2
    return x


def _mlstm_prompt_kernel(q_ref, kt_ref, v_ref, o_ref, gt_ref, bg_ref, ng_ref,
                         hh_ref, cst_ref, mst_ref, ct_ref, m_ref):
    c = pl.program_id(1)
    L = q_ref.shape[0]
    H, DQK, DV = MLSTM_HEADS, MLSTM_DQK, MLSTM_DV

    @pl.when(c == 0)
    def _():
        ct_ref[...] = jnp.zeros_like(ct_ref)
        m_ref[...] = jnp.full_like(m_ref, NEG_INF)

    gates = gt_ref[...] + bg_ref[...]
    ig = gates[:H]
    lf = _log_sigmoid(gates[H:])
    b = _lane_cumsum(lf)
    a = ig - b
    m0 = m_ref[...]
    big_m = jnp.maximum(m0, _lane_cummax(a))
    m = b + big_m
    cols = jnp.concatenate([big_m, m, jnp.zeros((128 - 2 * H, L), F32)], axis=0).T
    row_i = lax.broadcasted_iota(I32, (L, L), 0)
    col_i = lax.broadcasted_iota(I32, (L, L), 1)
    causal = col_i <= row_i
    ones_col = (lax.broadcasted_iota(I32, (L, DV), 1) == 0).astype(BF16)
    for h in range(H):
        q = q_ref[:, h * DQK:(h + 1) * DQK]
        kt = kt_ref[h * DQK:(h + 1) * DQK, :]
        v_ext = jnp.concatenate([v_ref[:, h * DV:(h + 1) * DV], ones_col], axis=1)
        big_m_col = cols[:, h:h + 1]
        m_col = cols[:, H + h:H + h + 1]
        w = jnp.exp(jnp.where(causal, a[h:h + 1, :] - big_m_col, NEG_INF))
        w_inter = jnp.exp(m0[h:h + 1, :] - big_m_col)
        s = _dot(q, kt) * w
        ct = ct_ref[h]
        tot = w_inter * _dot(q, ct.astype(BF16)) + _dot(s.astype(BF16), v_ext)
        num = tot[:, :DV]
        den = tot[:, DV:DV + 1]
        hv = num / jnp.maximum(jnp.abs(den), jnp.exp(-m_col))
        hv = _rms(hv, ng_ref[h:h + 1, :])
        gate = jax.nn.sigmoid(o_ref[:, h * DV:(h + 1) * DV])
        hh_ref[:, h * DV:(h + 1) * DV] = (hv * gate).astype(hh_ref.dtype)
        g_row = w[L - 1:L, :]
        decay = w_inter[L - 1:L, :]
        ct_ref[h] = decay * ct + _dot((kt.astype(F32) * g_row).astype(BF16), v_ext)
    m_ref[...] = m[:, L - 1:L]

    @pl.when(c == pl.num_programs(1) - 1)
    def _():
        cst_ref[0] = ct_ref[...]
        mst_ref[0] = m_ref[...]


def mlstm_prompt(q, kt, v, o, gt, b_gate, norm_g, bsz, seq):
    H, DQK, DV, L = MLSTM_HEADS, MLSTM_DQK, MLSTM_DV, MLSTM_CHUNK
    nc = seq // L
    n_rows = bsz * seq
    row = lambda b_, c_: (b_ * nc + c_, 0)
    col = lambda b_, c_: (b_, 0, c_)
    fixed = lambda b_, c_: (0, 0)
    hh, cst, mst = pl.pallas_call(
        _mlstm_prompt_kernel,
        grid=(bsz, nc),
        in_specs=[pl.BlockSpec((L, H * DQK), row), pl.BlockSpec((None, H * DQK, L), col),
                  pl.BlockSpec((L, H * DV), row), pl.BlockSpec((L, H * DV), row),
                  pl.BlockSpec((None, 2 * H, L), col), pl.BlockSpec((2 * H, 1), fixed),
                  pl.BlockSpec((H, DV), fixed)],
        out_specs=[pl.BlockSpec((L, H * DV), row),
                   pl.BlockSpec((1, H, DQK, 2 * DV), lambda b_, c_: (b_, 0, 0, 0)),
                   pl.BlockSpec((1, H, 1), lambda b_, c_: (b_, 0, 0))],
        out_shape=[jax.ShapeDtypeStruct((n_rows, H * DV), BF16),
                   jax.ShapeDtypeStruct((bsz, H, DQK, 2 * DV), F32),
                   jax.ShapeDtypeStruct((bsz, H, 1), F32)],
        scratch_shapes=[pltpu.VMEM((H, DQK, 2 * DV), F32), pltpu.VMEM((H, 1), F32)],
        compiler_params=_cparams(2, 32),
        name="mlstm_prompt",
    )(q, kt, v, o, gt, b_gate.reshape(2 * H, 1), norm_g)
    c_new = jnp.swapaxes(cst[..., :DV], 2, 3)
    n_new = cst[..., DV]
    return hh, c_new, n_new, mst[..., 0]


def _mlstm_sample_kernel(q_ref, k_ref, v_ref, vt_ref, o_ref, g_ref, bg_ref, ng_ref, c0_ref, n0_ref, m0_ref,
                         hh_ref, c_ref, n_ref, m_ref):
    H = MLSTM_HEADS
    gates = g_ref[0] + bg_ref[...]
    ig = gates[:H]
    lf = _log_sigmoid(gates[H:])
    m0 = m0_ref[0]
    m = jnp.maximum(lf + m0, ig)
    w_inter = jnp.exp(lf + m0 - m)
    w_intra = jnp.exp(ig - m)
    qb = q_ref[0]
    q32 = qb.astype(F32)
    k = k_ref[0]
    v = v_ref[0]
    s = jnp.sum(q32 * k.astype(BF16).astype(F32), axis=-1, keepdims=True) * w_intra
    n0 = n0_ref[0]
    den = w_inter * jnp.sum(q32 * n0.astype(BF16).astype(F32), axis=-1, keepdims=True) + s
    row_i = lax.broadcasted_iota(I32, (H, MLSTM_DV), 0)
    inter = jnp.zeros((H, MLSTM_DV), F32)
    for h in range(H):
        c0 = c0_ref[0, h]
        inter = jnp.where(row_i == h, _dot_t(qb, c0.astype(BF16)), inter)
        c_ref[0, h] = w_inter[h:h + 1, :] * c0 + (w_intra[h:h + 1, :] * vt_ref[0, :, h:h + 1]) * k[h:h + 1, :]
    num = w_inter * inter + s * v.astype(BF16).astype(F32)
    hv = num / jnp.maximum(jnp.abs(den), jnp.exp(-m))
    hv = _rms(hv, ng_ref[...])
    hh_ref[0] = (hv * jax.nn.sigmoid(o_ref[0])).astype(hh_ref.dtype)
    n_ref[0] = w_inter * n0 + w_intra * k
    m_ref[0] = m


def mlstm_sample(q, k, v, o, gates, b_gate, norm_g, c0, n0, m0):
    H, DQK, DV = MLSTM_HEADS, MLSTM_DQK, MLSTM_DV
    bsz = q.shape[0]
    q3 = q.reshape(bsz, H, DQK)
    k3 = k.reshape(bsz, H, DQK)
    v3 = v.reshape(bsz, H, DV)
    vt = jnp.swapaxes(v3, 1, 2)
    b3 = lambda *blk: pl.BlockSpec((1,) + blk, lambda i: (i,) + (0,) * len(blk))
    fixed = lambda *blk: pl.BlockSpec(blk, lambda i: (0,) * len(blk))
    hh, c_new, n_new, m_new = pl.pallas_call(
        _mlstm_sample_kernel,
        grid=(bsz,),
        in_specs=[b3(H, DQK), b3(H, DQK), b3(H, DV), b3(DV, H), b3(H, DV), b3(2 * H, 1), fixed(2 * H, 1),
                  fixed(H, DV), b3(H, DV, DQK), b3(H, DQK), b3(H, 1)],
        out_specs=[b3(H, DV), b3(H, DV, DQK), b3(H, DQK), b3(H, 1)],
        out_shape=[jax.ShapeDtypeStruct((bsz, H, DV), BF16), jax.ShapeDtypeStruct((bsz, H, DV, DQK), F32),
                   jax.ShapeDtypeStruct((bsz, H, DQK), F32), jax.ShapeDtypeStruct((bsz, H, 1), F32)],
        compiler_params=_cparams(1, 16),
        name="mlstm_sample",
    )(q3, k3, v3, vt, o.reshape(bsz, H, DV), gates.reshape(bsz, 2 * H, 1), b_gate.reshape(2 * H, 1), norm_g,
      c0, n0, m0.reshape(bsz, H, 1))
    return hh.reshape(bsz, H * DV), c_new, n_new, m_new[..., 0]


def mlstm_layer(xp, xs, g, w_in, b_gate, norm_g, w_out, c0, n0, m0, bsz, seq):
    H, DQK, DV = MLSTM_HEADS, MLSTM_DQK, MLSTM_DV
    e_q, e_k, e_v, e_o = H * DQK, 2 * H * DQK, 2 * H * DQK + H * DV, 2 * H * DQK + 2 * H * DV
    wq = w_in[:, :e_q] * (DQK ** -0.5)
    wk, wv, wo, wg = w_in[:, e_q:e_k], w_in[:, e_k:e_v], w_in[:, e_v:e_o], w_in[:, e_o:]
    q, kt, v, o, gt = norm_proj(xp, g, [(wq, False, [BF16]), (wk, True, [BF16]), (wv, False, [BF16]),
                                        (wo, False, [F32]), (wg, True, [F32])], tm=512, bsz=bsz)
    hh, c_p, n_p, m_p = mlstm_prompt(q, kt, v, o, gt, b_gate, norm_g, bsz, seq)
    yp = out_proj(hh, w_out, xp, tm=512)
    qs, ks, vs, os_, gs = norm_proj(xs, g, [(wq, False, [BF16]), (wk, False, [F32]), (wv, False, [F32]),
                                            (wo, False, [F32]), (wg, False, [F32])], tm=xs.shape[0])
    hhs, c_s, n_s, m_s = mlstm_sample(qs, ks, vs, os_, gs, b_gate, norm_g, c0, n0, m0)
    ys = out_proj(hhs, w_out, xs, tm=xs.shape[0])
    return yp, ys, (c_p, n_p, m_p), (c_s, n_s, m_s)


def _softplus(x):
    return jnp.maximum(x, 0.0) + jnp.log1p(jnp.exp(-jnp.abs(x)))


def _rglru_gates(conv, n, wa_ref, wx_ref, ba_ref, bx_ref, lam_ref):
    sl = slice(n * RG_BLOCK, (n + 1) * RG_BLOCK)
    cb = conv.astype(BF16)
    r = jax.nn.sigmoid(_dot(cb, wa_ref[n]) + ba_ref[:, sl])
    i = jax.nn.sigmoid(_dot(cb, wx_ref[n]) + bx_ref[:, sl])
    log_a = -RG_C * r * _softplus(-lam_ref[:, sl])
    a = jnp.exp(log_a)
    u = jnp.sqrt(jnp.tanh(-log_a) * (a * a + 1.0)) * (i * conv)
    return a, u


def _rglru_prompt_kernel(gate_ref, xb_ref, cw_ref, cb_ref, wa_ref, wx_ref, ba_ref, bx_ref, lam_ref,
                         y_ref, h_ref, tail_ref, prev_ref, hc_ref):
    c = pl.program_id(1)
    T = xb_ref.shape[0]

    @pl.when(c == 0)
    def _():
        prev_ref[...] = jnp.zeros_like(prev_ref)
        hc_ref[...] = jnp.zeros_like(hc_ref)

    row = lax.broadcasted_iota(I32, (T, RG_BLOCK), 0)
    for n in range(RG_BLOCKS):
        sl = slice(n * RG_BLOCK, (n + 1) * RG_BLOCK)
        x = xb_ref[:, sl]
        x_ext = jnp.concatenate([prev_ref[:, sl], x], axis=0)
        conv = cb_ref[:, sl] + cw_ref[RG_CONV - 1:RG_CONV, sl] * x
        for k in range(1, RG_CONV):
            conv = conv + cw_ref[RG_CONV - 1 - k:RG_CONV - k, sl] * pltpu.roll(x_ext, k, axis=0)[8:, :]
        a, u = _rglru_gates(conv, n, wa_ref, wx_ref, ba_ref, bx_ref, lam_ref)
        d = 1
        while d < T:
            keep = row >= d
            a_sh = jnp.where(keep, pltpu.roll(a, d, axis=0), 1.0)
            u_sh = jnp.where(keep, pltpu.roll(u, d, axis=0), 0.0)
            u = a * u_sh + u
            a = a * a_sh
            d *= 2
        h = a * hc_ref[:, sl] + u
        hc_ref[:, sl] = h[T - 1:T, :]
        y_ref[:, sl] = (h * jax.nn.gelu(gate_ref[:, sl])).astype(y_ref.dtype)
    prev_ref[...] = xb_ref[T - 8:T, :]

    @pl.when(c == pl.num_programs(1) - 1)
    def _():
        h_ref[0] = hc_ref[...]
        tail_ref[0] = xb_ref[T - 8:T, :]


def rglru_prompt(gate, xb, conv_w, conv_b, w_a, b_a, w_x, b_x, lam, bsz, seq, tt):
    W = RG_WIDTH
    nc = seq // tt
    row = lambda b_, c_: (b_ * nc + c_, 0)
    fixed2 = lambda b_, c_: (0, 0)
    fixed3 = lambda b_, c_: (0, 0, 0)
    y, h, tail = pl.pallas_call(
        _rglru_prompt_kernel,
        grid=(bsz, nc),
        in_specs=[pl.BlockSpec((tt, W), row), pl.BlockSpec((tt, W), row),
                  pl.BlockSpec((RG_CONV, W), fixed2), pl.BlockSpec((1, W), fixed2),
                  pl.BlockSpec((RG_BLOCKS, RG_BLOCK, RG_BLOCK), fixed3),
                  pl.BlockSpec((RG_BLOCKS, RG_BLOCK, RG_BLOCK), fixed3),
                  pl.BlockSpec((1, W), fixed2), pl.BlockSpec((1, W), fixed2), pl.BlockSpec((1, W), fixed2)],
        out_specs=[pl.BlockSpec((tt, W), row), pl.BlockSpec((1, 1, W), lambda b_, c_: (b_, 0, 0)),
                   pl.BlockSpec((1, 8, W), lambda b_, c_: (b_, 0, 0))],
        out_shape=[jax.ShapeDtypeStruct((bsz * seq, W), BF16), jax.ShapeDtypeStruct((bsz, 1, W), F32),
                   jax.ShapeDtypeStruct((bsz, 8, W), F32)],
        scratch_shapes=[pltpu.VMEM((8, W), F32), pltpu.VMEM((1, W), F32)],
        compiler_params=_cparams(2, 32),
        name="rglru_prompt",
    )(gate, xb, conv_w, conv_b.reshape(1, W), w_a.astype(BF16), w_x.astype(BF16),
      b_a.reshape(1, W), b_x.reshape(1, W), lam.reshape(1, W))
    return y, h[:, 0], tail[:, 8 - (RG_CONV - 1):]


def _rglru_sample_kernel(gate_ref, xb_ref, c0_ref, c1_ref, c2_ref, h0_ref, cw_ref, cb_ref, wa_ref, wx_ref,
                         ba_ref, bx_ref, lam_ref, y_ref, h_ref):
    taps = (c0_ref, c1_ref, c2_ref, xb_ref)
    for n in range(RG_BLOCKS):
        sl = slice(n * RG_BLOCK, (n + 1) * RG_BLOCK)
        conv = cb_ref[:, sl]
        for j in range(RG_CONV):
            conv = conv + cw_ref[j:j + 1, sl] * taps[j][:, sl]
        a, u = _rglru_gates(conv, n, wa_ref, wx_ref, ba_ref, bx_ref, lam_ref)
        h = a * h0_ref[:, sl] + u
        h_ref[:, sl] = h
        y_ref[:, sl] = (h * jax.nn.gelu(gate_ref[:, sl])).astype(y_ref.dtype)


def rglru_sample(gate, xb, conv0, h0, conv_w, conv_b, w_a, b_a, w_x, b_x, lam):
    W = RG_WIDTH
    bsz = gate.shape[0]
    full2 = lambda r: pl.BlockSpec((r, W), lambda i: (0, 0))
    full3 = pl.BlockSpec((RG_BLOCKS, RG_BLOCK, RG_BLOCK), lambda i: (0, 0, 0))
    y, h = pl.pallas_call(
        _rglru_sample_kernel,
        grid=(1,),
        in_specs=[full2(bsz)] * 6 + [full2(RG_CONV), full2(1), full3, full3, full2(1), full2(1), full2(1)],
        out_specs=[full2(bsz), full2(bsz)],
        out_shape=[jax.ShapeDtypeStruct((bsz, W), BF16), jax.ShapeDtypeStruct((bsz, W), F32)],
        compiler_params=_cparams(1, 16),
        name="rglru_sample",
    )(gate, xb, conv0[:, 0], conv0[:, 1], conv0[:, 2], h0, conv_w, conv_b.reshape(1, W),
      w_a.astype(BF16), w_x.astype(BF16), b_a.reshape(1, W), b_x.reshape(1, W), lam.reshape(1, W))
    new_conv = jnp.concatenate([conv0[:, 1:], xb[:, None, :]], axis=1)
    return y, h, new_conv


def rglru_layer(xp, xs, g, w_in, conv_w, conv_b, w_a, b_a, w_x, b_x, lam, w_out, h0, conv0, bsz, seq):
    W = RG_WIDTH
    segs = [(w_in[:, :W], False, [F32]), (w_in[:, W:], False, [F32])]
    gate, xb = norm_proj(xp, g, segs, tm=512)
    y, h_p, conv_p = rglru_prompt(gate, xb, conv_w, conv_b, w_a, b_a, w_x, b_x, lam, bsz, seq, tt=RG_TT)
    yp = out_proj(y, w_out, xp, tm=512)
    gate_s, xb_s = norm_proj(xs, g, segs, tm=xs.shape[0])
    y_s, h_s, conv_s = rglru_sample(gate_s, xb_s, conv0, h0, conv_w, conv_b, w_a, b_a, w_x, b_x, lam)
    ys = out_proj(y_s, w_out, xs, tm=xs.shape[0])
    return yp, ys, (h_p, conv_p), (h_s, conv_s)


ATT_HEADS = 16
ATT_DH = 64
ATT_PAIRS = ATT_HEADS // 2
LANES = 128


def _split_heads(q_ref, qh_ref):
    tq = q_ref.shape[0]
    low_half = lax.broadcasted_iota(I32, (tq, LANES), 1) < ATT_DH
    for j in range(ATT_PAIRS):
        qp = q_ref[:, j * LANES:(j + 1) * LANES]
        zero = jnp.zeros_like(qp)
        qh_ref[2 * j] = jnp.where(low_half, qp, zero)
        qh_ref[2 * j + 1] = jnp.where(low_half, zero, qp)


def _lane_fold(x, op):
    out = x[:, :LANES]
    for c in range(1, x.shape[1] // LANES):
        out = op(out, x[:, c * LANES:(c + 1) * LANES])
    return out


def _flash_update(h, s, v_pair, m_ref, l_ref, acc_ref):
    reps = s.shape[1] // LANES
    m_old = m_ref[h]
    m_new = jnp.maximum(m_old, jnp.max(_lane_fold(s, jnp.maximum), axis=-1, keepdims=True))
    m_ref[h] = m_new
    alpha = jnp.exp2(m_old - m_new)
    p = jnp.exp2(s - jnp.concatenate([m_new] * reps, axis=1))
    l_ref[h] = alpha * l_ref[h] + _lane_fold(p, jnp.add)
    acc_ref[h] = alpha * acc_ref[h] + _dot(p.astype(BF16), v_pair)


def _flash_finish(o_ref, l_ref, acc_ref):
    tq = o_ref.shape[0]
    low_half = lax.broadcasted_iota(I32, (tq, LANES), 1) < ATT_DH
    for j in range(ATT_PAIRS):
        lo = acc_ref[2 * j] / jnp.sum(l_ref[2 * j], axis=-1, keepdims=True)
        hi = acc_ref[2 * j + 1] / jnp.sum(l_ref[2 * j + 1], axis=-1, keepdims=True)
        o_ref[:, j * LANES:(j + 1) * LANES] = jnp.where(low_half, lo, hi).astype(o_ref.dtype)


def _flash_init(m_ref, l_ref, acc_ref):
    m_ref[...] = jnp.full_like(m_ref, F32_LOWEST)
    l_ref[...] = jnp.zeros_like(l_ref)
    acc_ref[...] = jnp.zeros_like(acc_ref)


def _fox_gate_kernel(ft_ref, bft_ref, ct_ref, logf_ref, carry_ref):
    c = pl.program_id(1)
    ts = ft_ref.shape[1]

    @pl.when(c == 0)
    def _():
        carry_ref[...] = jnp.zeros_like(carry_ref)

    logf = _log_sigmoid(ft_ref[...] + bft_ref[...])
    logf_ref[...] = logf
    cs = _lane_cumsum(logf) + carry_ref[...]
    ct_ref[...] = cs
    carry_ref[...] = cs[:, ts - 1:ts]


def fox_gates(ft, b_f, ts):
    bsz, H, seq = ft.shape
    blk = pl.BlockSpec((None, H, ts), lambda b_, c_: (b_, 0, c_))
    return pl.pallas_call(
        _fox_gate_kernel,
        grid=(bsz, seq // ts),
        in_specs=[blk, pl.BlockSpec((H, 1), lambda b_, c_: (0, 0))],
        out_specs=[blk, blk],
        out_shape=[jax.ShapeDtypeStruct(ft.shape, F32), jax.ShapeDtypeStruct(ft.shape, F32)],
        scratch_shapes=[pltpu.VMEM((H, 1), F32)],
        compiler_params=_cparams(2, 16),
        name="fox_gates",
    )(ft, b_f.reshape(H, 1))


def _fox_prompt_kernel(q_ref, kt_ref, v_ref, ct_ref, o_ref, qh_ref, m_ref, l_ref, acc_ref):
    qi = pl.program_id(1)
    tq = q_ref.shape[0]
    qs = pl.multiple_of(qi * tq, tq)
    _split_heads(q_ref, qh_ref)
    _flash_init(m_ref, l_ref, acc_ref)
    c_q0 = ct_ref[:, pl.ds(qs, LANES)][:, 0:1]
    causal = (lax.broadcasted_iota(I32, (tq, tq), 1) <= lax.broadcasted_iota(I32, (tq, tq), 0))

    def block(kb, diagonal):
        ks = pl.multiple_of(kb * tq, tq)
        decay = (c_q0 - ct_ref[:, pl.ds(ks, tq)]) * LOG2E
        for h in range(ATT_HEADS):
            j = h // 2
            s = _dot(qh_ref[h], kt_ref[j * LANES:(j + 1) * LANES, pl.ds(ks, tq)]) + decay[h:h + 1, :]
            if diagonal:
                s = jnp.where(causal, s, NEG_INF)
            _flash_update(h, s, v_ref[pl.ds(ks, tq), j * LANES:(j + 1) * LANES], m_ref, l_ref, acc_ref)

    def far(kb, carry):
        block(kb, False)
        return carry

    lax.fori_loop(0, qi, far, 0)
    block(qi, True)
    _flash_finish(o_ref, l_ref, acc_ref)


def fox_prompt_attend(q, kt, v, ct, bsz, seq, tq):
    hd = ATT_HEADS * ATT_DH
    nq = seq // tq
    once = pl.Buffered(1)
    return pl.pallas_call(
        _fox_prompt_kernel,
        grid=(bsz, nq),
        in_specs=[pl.BlockSpec((tq, hd), lambda b_, i: (b_ * nq + i, 0)),
                  pl.BlockSpec((None, hd, seq), lambda b_, i: (b_, 0, 0), pipeline_mode=once),
                  pl.BlockSpec((seq, hd), lambda b_, i: (b_, 0), pipeline_mode=once),
                  pl.BlockSpec((None, ATT_HEADS, seq), lambda b_, i: (b_, 0, 0), pipeline_mode=once)],
        out_specs=pl.BlockSpec((tq, hd), lambda b_, i: (b_ * nq + i, 0)),
        out_shape=jax.ShapeDtypeStruct((bsz * seq, hd), BF16),
        scratch_shapes=[pltpu.VMEM((ATT_HEADS, tq, LANES), BF16), pltpu.VMEM((ATT_HEADS, tq, LANES), F32),
                        pltpu.VMEM((ATT_HEADS, tq, LANES), F32), pltpu.VMEM((ATT_HEADS, tq, LANES), F32)],
        compiler_params=_cparams(2, 56),
        name="fox_prompt",
    )(q, kt, v, ct)


PAGES_PER_STEP = 8


def _paged_attend_kernel(pt_ref, qb_ref, qr_ref, kn_ref, vn_ref, *refs, mode):
    pp = PAGES_PER_STEP
    k_refs, v_refs = refs[:pp], refs[pp:2 * pp]
    refs = refs[2 * pp:]
    if mode == "fox":
        lf_refs, lfn_ref = refs[:pp], refs[pp]
        o_ref, m_ref, l_ref, acc_ref, cc_ref = refs[pp + 1:]
    else:
        mask_ref, maskn_ref, bias_ref, biasn_ref = refs[:4]
        o_ref, m_ref, l_ref, acc_ref = refs[4:]
    j = pl.program_id(1)
    H = ATT_HEADS

    @pl.when(j == 0)
    def _():
        _flash_init(m_ref, l_ref, acc_ref)
        if mode == "fox":
            cc_ref[...] = jnp.zeros_like(cc_ref)

    head_row = lax.broadcasted_iota(I32, (H, PAGE_SIZE), 0)

    def softmax_step(s):
        m_old = m_ref[...]
        m_new = jnp.maximum(m_old, jnp.max(s, axis=-1, keepdims=True))
        m_ref[...] = m_new
        alpha = jnp.exp(m_old - m_new)
        p = jnp.exp(s - m_new)
        l_ref[...] = alpha * l_ref[...] + jnp.sum(p, axis=-1, keepdims=True)
        return alpha, p

    logits = []
    cc = cc_ref[...] if mode == "fox" else None
    for p_i in range(pp):
        s = jnp.zeros((H, PAGE_SIZE), F32)
        for h in range(H):
            part = jnp.sum(k_refs[p_i][0, h] * qb_ref[0, h], axis=0, keepdims=True)
            s = jnp.where(head_row == h, part, s)
        if mode == "fox":
            c = _lane_cumsum(lf_refs[p_i][0]) + cc
            cc = c[:, PAGE_SIZE - 1:PAGE_SIZE]
            s = s - c
        else:
            sl = slice(p_i * PAGE_SIZE, (p_i + 1) * PAGE_SIZE)
            s = s + mask_ref[0][:, sl] + bias_ref[:, sl]
        logits.append(s)
    if mode == "fox":
        cc_ref[...] = cc
    alpha, p = softmax_step(jnp.concatenate(logits, axis=1))
    for h in range(H):
        a = acc_ref[h] * alpha[h:h + 1, :]
        for p_i in range(pp):
            a = a + v_refs[p_i][0, h] * p[h:h + 1, p_i * PAGE_SIZE:(p_i + 1) * PAGE_SIZE]
        acc_ref[h] = a

    @pl.when(j == pl.num_programs(1) - 1)
    def _():
        s = jnp.sum(qr_ref[0] * kn_ref[0], axis=-1, keepdims=True)
        if mode == "fox":
            s = s - (cc_ref[...] + lfn_ref[0])
        else:
            s = s + maskn_ref[0][:, 0:1] + biasn_ref[:, 0:1]
        alpha, p = softmax_step(s)
        out_row = lax.broadcasted_iota(I32, (H, LANES), 0)
        out = jnp.zeros((H, LANES), F32)
        pad = jnp.zeros((LANES - ATT_DH, PAGE_SIZE), F32)
        for h in range(H):
            a_t = jnp.concatenate([acc_ref[h] * alpha[h:h + 1, :], pad], axis=0).T
            out = jnp.where(out_row == h, jnp.sum(a_t, axis=0, keepdims=True), out)
        out = (out[:, :ATT_DH] + p * vn_ref[0]) / l_ref[...]
        o_ref[0] = out.astype(o_ref.dtype)


def paged_attend(mode, page_table, q, k_new, v_new, pool_k, pool_v, extras):
    pp = PAGES_PER_STEP
    bsz, hd = q.shape
    H, DH = ATT_HEADS, ATT_DH
    n_pages = page_table.shape[1]
    past = n_pages * PAGE_SIZE
    assert n_pages % pp == 0
    pool_kt = jnp.transpose(pool_k, (0, 2, 3, 1))
    pool_vt = jnp.transpose(pool_v, (0, 2, 3, 1))
    q3 = q.astype(F32).reshape(bsz, H, DH)
    q_lanes = jnp.broadcast_to(q3[..., None], (bsz, H, DH, PAGE_SIZE))
    row3 = pl.BlockSpec((1, H, DH), lambda b_, j, pt: (b_, 0, 0))
    page = lambda p_i, *blk: pl.BlockSpec((1,) + blk, lambda b_, j, pt, p_i=p_i: (pt[b_, j * pp + p_i],) + (0,) * len(blk))
    in_specs = [pl.BlockSpec((1, H, DH, PAGE_SIZE), lambda b_, j, pt: (b_, 0, 0, 0)), row3, row3, row3]
    in_specs += [page(p_i, H, DH, PAGE_SIZE) for p_i in range(pp)] * 2
    args = [q_lanes, q3, k_new.reshape(bsz, H, DH), v_new.reshape(bsz, H, DH)]
    args += [pool_kt] * pp + [pool_vt] * pp
    scratch = [pltpu.VMEM((H, 1), F32), pltpu.VMEM((H, 1), F32), pltpu.VMEM((H, DH, PAGE_SIZE), F32)]
    if mode == "fox":
        pool_lft, lf_new = extras
        in_specs += [page(p_i, ATT_HEADS, PAGE_SIZE) for p_i in range(pp)]
        in_specs += [pl.BlockSpec((1, ATT_HEADS, 1), lambda b_, j, pt: (b_, 0, 0))]
        args += [pool_lft] * pp + [lf_new]
        scratch += [pltpu.VMEM((ATT_HEADS, 1), F32)]
    else:
        mask, bias = extras
        in_specs += [pl.BlockSpec((1, 1, pp * PAGE_SIZE), lambda b_, j, pt: (b_, 0, j)),
                     pl.BlockSpec((1, 1, LANES), lambda b_, j, pt: (b_, 0, past // LANES)),
                     pl.BlockSpec((ATT_HEADS, pp * PAGE_SIZE), lambda b_, j, pt: (0, j)),
                     pl.BlockSpec((ATT_HEADS, LANES), lambda b_, j, pt: (0, past // LANES))]
        args += [mask, mask, bias, bias]
    out = pl.pallas_call(
        functools.partial(_paged_attend_kernel, mode=mode),
        grid_spec=pltpu.PrefetchScalarGridSpec(
            num_scalar_prefetch=1, grid=(bsz, n_pages // pp), in_specs=in_specs,
            out_specs=pl.BlockSpec((1, H, DH), lambda b_, j, pt: (b_, 0, 0)), scratch_shapes=scratch),
        out_shape=jax.ShapeDtypeStruct((bsz, H, DH), BF16),
        compiler_params=_cparams(2, 48),
        name="paged_attend_" + mode,
    )(page_table, *args)
    return out.reshape(bsz, hd)


def fox_layer(xp, xs, g, w_in, b_f, w_out, pool_k, pool_v, pool_logf, page_table, bsz, seq):
    hd = FOX_HEADS * FOX_DH
    wq = w_in[:, :hd] * (FOX_DH ** -0.5)
    wk, wv, wf = w_in[:, hd:2 * hd], w_in[:, 2 * hd:3 * hd], w_in[:, 3 * hd:]
    q, ktf, kt, vb, vtf, ft = norm_proj(xp, g, [(wq * LOG2E, False, [BF16]), (wk, True, [F32, BF16]),
                                                (wv, False, [BF16]), (wv, True, [F32]), (wf, True, [F32])],
                                        tm=512, bsz=bsz)
    ct, logft = fox_gates(ft, b_f, ts=min(seq, 2048))
    o = fox_prompt_attend(q, kt, vb, ct, bsz, seq, tq=FOX_TQ)
    yp = out_proj(o, w_out, xp, tm=512)
    k = jnp.transpose(ktf.reshape(bsz, FOX_HEADS, FOX_DH, seq), (0, 3, 1, 2))
    v = jnp.transpose(vtf.reshape(bsz, FOX_HEADS, FOX_DH, seq), (0, 3, 1, 2))
    logf = jnp.swapaxes(logft, 1, 2)
    dbs = xs.shape[0]
    qs, ks, vs, fs = norm_proj(xs, g, [(wq, False, [BF16]), (wk, False, [F32]), (wv, False, [F32]),
                                       (wf, False, [F32])], tm=dbs)
    lf_s = fox_logf_rows(fs, b_f)
    os_ = paged_attend("fox", page_table, qs, ks, vs, pool_k, pool_v,
                       (jnp.swapaxes(pool_logf, 1, 2), lf_s.reshape(dbs, FOX_HEADS, 1)))
    ys = out_proj(os_, w_out, xs, tm=dbs)
    shp = lambda t, n: t.reshape(n, -1, FOX_HEADS, FOX_DH)
    return (yp, ys, (k, v, logf), (shp(ks, dbs), shp(vs, dbs), lf_s.reshape(dbs, 1, FOX_HEADS)))


def _ordered_key(x):
    b = lax.bitcast_convert_type(x + 0.0, I32)
    return b ^ ((b >> 31) & INT_MAX)


def _as_f32(x):
    return lax.bitcast_convert_type(x, F32)


def _as_i32(x):
    return lax.bitcast_convert_type(x, I32)


FIELD_BITS = 15
PACKED_UNROLL = 4


def _packed_prefix_search(sc_ref, pk_ref, n_blocks, rows, k_f):
    guard = 1 << FIELD_BITS
    field_max = guard - 1
    both = (1 << 16) | 1
    wide = 2 * LANES

    def unsigned_order(kb):
        key = _as_i32(sc_ref[:, pl.ds(pl.multiple_of(kb * wide, wide), wide)])
        return key ^ INT_MIN

    def store_packed(kb, f):
        pk_ref[:, pl.ds(pl.multiple_of(kb * LANES, LANES), LANES)] = ((f[:, :LANES] | guard) << 16) | (f[:, LANES:] | guard)

    n_groups = (n_blocks + PACKED_UNROLL - 1) // PACKED_UNROLL

    def pad_tail():
        def body(kb, carry):
            store_packed(kb, jnp.zeros((rows, wide), I32))
            return carry
        lax.fori_loop(n_blocks, n_groups * PACKED_UNROLL, body, 0)

    def count_ge(cand):
        cand2 = (cand << 16) | cand

        def body(g, acc):
            for u in range(PACKED_UNROLL):
                col = pl.multiple_of((g * PACKED_UNROLL + u) * LANES, LANES)
                acc = acc + (lax.shift_right_logical(pk_ref[:, pl.ds(col, LANES)] - cand2, FIELD_BITS) & both)
            return acc
        acc = lax.fori_loop(0, n_groups, body, jnp.zeros((rows, LANES), I32))
        per_lane = (acc & 0xFFFF) + lax.shift_right_logical(acc, 16)
        return jnp.sum(per_lane.astype(F32), axis=-1, keepdims=True)

    def search():
        def step(i, best):
            cand = best | (1 << (FIELD_BITS - 1 - i))
            return jnp.where(count_ge(cand) >= k_f, cand, best)
        return lax.fori_loop(0, FIELD_BITS, step, jnp.zeros((rows, 1), I32))

    def build_high(kb, carry):
        store_packed(kb, lax.shift_right_logical(unsigned_order(kb), 32 - FIELD_BITS))
        return carry

    lax.fori_loop(0, n_blocks, build_high, 0)
    pad_tail()
    high = search()

    def build_mid(kb, carry):
        u = unsigned_order(kb)
        f_high = lax.shift_right_logical(u, 32 - FIELD_BITS)
        f_mid = lax.shift_right_logical(u, 32 - 2 * FIELD_BITS) & field_max
        store_packed(kb, jnp.where(f_high > high, field_max, jnp.where(f_high < high, 0, f_mid)))
        return carry

    lax.fori_loop(0, n_blocks, build_mid, 0)
    mid = search()
    return ((high << (32 - FIELD_BITS)) | (mid << (32 - 2 * FIELD_BITS))) ^ INT_MIN


def _topk_to_mask(sc_ref, n_blocks, width, rows, topk, pk_ref=None):
    lane = lax.broadcasted_iota(I32, (rows, width), 1)
    k_f = float(topk)

    def count(pred):
        def body(kb, acc):
            ks = pl.multiple_of(kb * width, width)
            hit = jnp.where(pred(_as_i32(sc_ref[:, pl.ds(ks, width)]), ks), 1.0, 0.0)
            return acc + _lane_fold(hit, jnp.add)
        acc = lax.fori_loop(0, n_blocks, body, jnp.zeros((rows, LANES), F32))
        return jnp.sum(acc, axis=-1, keepdims=True)

    def bit_step(i, t):
        cand = t | (1 << (30 - i))
        return jnp.where(count(lambda key, ks: key >= cand) >= k_f, cand, t)

    if pk_ref is None:
        t0 = jnp.where(count(lambda key, ks: key >= 0) >= k_f, 0, INT_MIN).astype(I32)
        t = lax.fori_loop(0, 31, bit_step, t0)
    else:
        assert width == 2 * LANES
        t = lax.fori_loop(2 * FIELD_BITS - 1, 31, bit_step, _packed_prefix_search(sc_ref, pk_ref, n_blocks, rows, k_f))
    need = k_f - count(lambda key, ks: key > t)
    n_eq = count(lambda key, ks: key == t)

    def tie_search(_):
        def idx_step(i, x):
            cand = x | (1 << (30 - i))
            below = count(lambda key, ks: (key == t) & (lane + ks < cand))
            return jnp.where(below < need, cand, x)
        return lax.fori_loop(0, 31, idx_step, jnp.zeros((rows, 1), I32))

    x = lax.cond(jnp.max(n_eq - need) > 0.0, tie_search, lambda _: jnp.full((rows, 1), INT_MAX, I32), 0)

    def write(kb, carry):
        ks = pl.multiple_of(kb * width, width)
        key = _as_i32(sc_ref[:, pl.ds(ks, width)])
        sel = ((key > t) | ((key == t) & (lane + ks <= x))) & (key > KEY_NEG_INF)
        sc_ref[:, pl.ds(ks, width)] = jnp.where(sel, 0.0, NEG_INF)
        return carry

    lax.fori_loop(0, n_blocks, write, 0)


DSA_TK = 2 * LANES
DSA_FAR_TK = 2 * LANES


def _dsa_prompt_kernel(q_ref, qi_ref, wi_ref, kit_ref, kt_ref, v_ref, bias_ref, o_ref,
                       sc_ref, pk_ref, qh_ref, qih_ref, wib_ref, m_ref, l_ref, acc_ref, *, topk):
    blk = pl.program_id(1)
    tq = q_ref.shape[0]
    tk = DSA_TK
    qs = blk * tq
    n_wide = (blk + 2) // 2
    row = lax.broadcasted_iota(I32, (tq, tk), 0)
    lane = lax.broadcasted_iota(I32, (tq, tk), 1)

    low_half = lax.broadcasted_iota(I32, (tq, LANES), 1) < IDX_DIM
    for j in range(IDX_HEADS // 2):
        qp = qi_ref[:, j * LANES:(j + 1) * LANES]
        zero = jnp.zeros_like(qp)
        qih_ref[2 * j] = jnp.where(low_half, qp, zero)
        qih_ref[2 * j + 1] = jnp.where(low_half, zero, qp)
    wi = wi_ref[...] * IDX_SCALE
    for h in range(IDX_HEADS):
        wib_ref[h] = jnp.broadcast_to(wi[:, h:h + 1], (tq, tk))

    def score_block(kb, carry):
        ks = pl.multiple_of(kb * tk, tk)
        kit = kit_ref[:, pl.ds(ks, tk)]
        sc = jnp.zeros((tq, tk), F32)
        for h in range(IDX_HEADS):
            sc = sc + wib_ref[h] * jnp.maximum(_dot(qih_ref[h], kit), 0.0)
        sc = jnp.where(lane + ks <= row + qs, sc, NEG_INF)
        sc_ref[:, pl.ds(ks, tk)] = _as_f32(_ordered_key(sc))
        return carry

    lax.fori_loop(0, n_wide, score_block, 0)

    _topk_to_mask(sc_ref, n_wide, tk, tq, topk, pk_ref)

    _split_heads(q_ref, qh_ref)
    _flash_init(m_ref, l_ref, acc_ref)

    def block(ks, width, bias_off):
        mask = sc_ref[:, pl.ds(ks, width)]
        for h in range(ATT_HEADS):
            j = h // 2
            s = _dot(qh_ref[h], kt_ref[j * LANES:(j + 1) * LANES, pl.ds(ks, width)]) + mask
            if bias_off is not None:
                s = s + bias_ref[h, :, bias_off:bias_off + width]
            _flash_update(h, s, v_ref[pl.ds(ks, width), j * LANES:(j + 1) * LANES], m_ref, l_ref, acc_ref)

    n_far = jnp.maximum(blk - 1, 0)

    per_far = DSA_FAR_TK // LANES

    def far(kb, carry):
        block(pl.multiple_of(kb * DSA_FAR_TK, DSA_FAR_TK), DSA_FAR_TK, None)
        return carry

    def far_rest(kb, carry):
        block(pl.multiple_of(kb * LANES, LANES), LANES, None)
        return carry

    lax.fori_loop(0, n_far // per_far, far, 0)
    lax.fori_loop((n_far // per_far) * per_far, n_far, far_rest, 0)

    @pl.when(blk >= 1)
    def _():
        block(pl.multiple_of((blk - 1) * LANES, LANES), 2 * LANES, 0)

    @pl.when(blk == 0)
    def _():
        block(0, LANES, LANES)

    _flash_finish(o_ref, l_ref, acc_ref)


def _t5_bucket_table(max_dist):
    n = np.arange(max_dist + 1)
    nf = np.maximum(n, 1).astype(np.float32)
    scale = np.float32((T5_BUCKETS - T5_MAX_EXACT) / math.log(T5_MAX_DISTANCE / T5_MAX_EXACT))
    large = T5_MAX_EXACT + (np.log(nf / np.float32(T5_MAX_EXACT)) * scale).astype(np.int32)
    large = np.minimum(large, T5_BUCKETS - 1)
    return np.where(n < T5_MAX_EXACT, n, large).astype(np.int32)


def dsa_prompt_attend(q, qi, wi, kit2, kt, v, rel_bias, bsz, seq):
    tq = LANES
    hd = ATT_HEADS * ATT_DH
    nq = seq // tq
    topk = min(DSA_TOPK, seq // 4)
    packed_blocks = -(-(seq // DSA_TK) // PACKED_UNROLL) * PACKED_UNROLL
    buckets = _t5_bucket_table(seq)
    assert np.all(buckets[tq:] == buckets[tq])
    dist = np.arange(tq)[:, None] + tq - np.arange(2 * tq)[None, :]
    one_hot = jax.nn.one_hot(buckets[np.maximum(dist, 0)].reshape(-1), T5_BUCKETS, dtype=F32)
    near = jnp.dot(one_hot, rel_bias, precision=lax.Precision.HIGHEST) - rel_bias[buckets[tq]]
    near = jnp.transpose(near.reshape(tq, 2 * tq, ATT_HEADS) * LOG2E, (2, 0, 1)).astype(F32)
    once = pl.Buffered(1)
    return pl.pallas_call(
        functools.partial(_dsa_prompt_kernel, topk=topk),
        grid=(bsz, nq),
        in_specs=[pl.BlockSpec((tq, hd), lambda b_, i: (b_ * nq + i, 0)),
                  pl.BlockSpec((tq, IDX_HEADS * IDX_DIM), lambda b_, i: (b_ * nq + i, 0)),
                  pl.BlockSpec((tq, IDX_HEADS), lambda b_, i: (b_ * nq + i, 0)),
                  pl.BlockSpec((None, 2 * IDX_DIM, seq), lambda b_, i: (b_, 0, 0), pipeline_mode=once),
                  pl.BlockSpec((None, hd, seq), lambda b_, i: (b_, 0, 0), pipeline_mode=once),
                  pl.BlockSpec((seq, hd), lambda b_, i: (b_, 0), pipeline_mode=once),
                  pl.BlockSpec((ATT_HEADS, tq, 2 * tq), lambda b_, i: (0, 0, 0), pipeline_mode=once)],
        out_specs=pl.BlockSpec((tq, hd), lambda b_, i: (b_ * nq + i, 0)),
        out_shape=jax.ShapeDtypeStruct((bsz * seq, hd), BF16),
        scratch_shapes=[pltpu.VMEM((tq, seq), F32), pltpu.VMEM((tq, packed_blocks * LANES), I32),
                        pltpu.VMEM((ATT_HEADS, tq, LANES), BF16),
                        pltpu.VMEM((IDX_HEADS, tq, LANES), BF16), pltpu.VMEM((IDX_HEADS, tq, DSA_TK), F32),
                        pltpu.VMEM((ATT_HEADS, tq, LANES), F32), pltpu.VMEM((ATT_HEADS, tq, LANES), F32),
                        pltpu.VMEM((ATT_HEADS, tq, LANES), F32)],
        compiler_params=_cparams(2, 56),
        name="dsa_prompt",
    )(q, qi, wi, kit2, kt, v, near)


def _dsa_sample_score_kernel(pt_ref, qi_ref, wi_ref, kin_ref, *refs):
    pp = PAGES_PER_STEP
    ki_refs = refs[:pp]
    sc_ref, scn_ref = refs[pp:]
    qi = qi_ref[0]
    wi = wi_ref[0] * IDX_SCALE
    for p_i in range(pp):
        d = jnp.maximum(_dot(qi, ki_refs[p_i][0].astype(BF16)), 0.0)
        sc_ref[0, :, p_i * PAGE_SIZE:(p_i + 1) * PAGE_SIZE] = jnp.sum(wi * d, axis=0, keepdims=True)
    d_new = jnp.sum(qi.astype(F32) * kin_ref[0].astype(BF16).astype(F32), axis=-1, keepdims=True)
    s_new = jnp.sum(wi * jnp.maximum(d_new, 0.0), axis=0, keepdims=True)
    scn_ref[0] = jnp.broadcast_to(s_new, (1, LANES))


def dsa_sample_scores(page_table, qi, wi, ki_new, pool_ki):
    pp = PAGES_PER_STEP
    bsz = qi.shape[0]
    n_pages = page_table.shape[1]
    past = n_pages * PAGE_SIZE
    pool_kit = jnp.swapaxes(pool_ki, 1, 2)
    page = lambda p_i: pl.BlockSpec((1, IDX_DIM, PAGE_SIZE), lambda b_, j, pt, p_i=p_i: (pt[b_, j * pp + p_i], 0, 0))
    sc, scn = pl.pallas_call(
        _dsa_sample_score_kernel,
        grid_spec=pltpu.PrefetchScalarGridSpec(
            num_scalar_prefetch=1, grid=(bsz, n_pages // pp),
            in_specs=[pl.BlockSpec((1, IDX_HEADS, IDX_DIM), lambda b_, j, pt: (b_, 0, 0)),
                      pl.BlockSpec((1, IDX_HEADS, 1), lambda b_, j, pt: (b_, 0, 0)),
                      pl.BlockSpec((1, 1, IDX_DIM), lambda b_, j, pt: (b_, 0, 0))] + [page(p_i) for p_i in range(pp)],
            out_specs=[pl.BlockSpec((1, 1, pp * PAGE_SIZE), lambda b_, j, pt: (b_, 0, j)),
                       pl.BlockSpec((1, 1, LANES), lambda b_, j, pt: (b_, 0, 0))]),
        out_shape=[jax.ShapeDtypeStruct((bsz, 1, past), F32), jax.ShapeDtypeStruct((bsz, 1, LANES), F32)],
        compiler_params=_cparams(2, 16),
        name="dsa_sample_scores",
    )(page_table, qi.reshape(bsz, IDX_HEADS, IDX_DIM), wi.reshape(bsz, IDX_HEADS, 1),
      ki_new.reshape(bsz, 1, IDX_DIM), *([pool_kit] * pp))
    pad = jnp.full((bsz, LANES - 1), NEG_INF, F32)
    return jnp.concatenate([sc[:, 0], scn[:, 0, :1], pad], axis=1)


def _dsa_sample_select_kernel(sc_ref, mask_ref, *, topk):
    rows, width = sc_ref.shape
    mask_ref[...] = _as_f32(_ordered_key(sc_ref[...]))
    _topk_to_mask(mask_ref, width // LANES, LANES, rows, topk)


def dsa_sample_select(scores, topk):
    rows, width = scores.shape
    return pl.pallas_call(
        functools.partial(_dsa_sample_select_kernel, topk=topk),
        grid=(1,),
        in_specs=[pl.BlockSpec((rows, width), lambda i: (0, 0))],
        out_specs=pl.BlockSpec((rows, width), lambda i: (0, 0)),
        out_shape=jax.ShapeDtypeStruct((rows, width), F32),
        compiler_params=_cparams(1, 16),
        name="dsa_sample_select",
    )(scores)


def dsa_layer(xp, xs, g, w_in, w_out, rel_bias, pool_k, pool_v, pool_ki, page_table, bsz, seq):
    hd = DSA_HEADS * DSA_DH
    e_qi, e_ki = 3 * hd + IDX_HEADS * IDX_DIM, 3 * hd + IDX_HEADS * IDX_DIM + IDX_DIM
    wq = w_in[:, :hd] * (DSA_DH ** -0.5)
    wk, wv = w_in[:, hd:2 * hd], w_in[:, 2 * hd:3 * hd]
    wqi, wki, wwi = w_in[:, 3 * hd:e_qi], w_in[:, e_qi:e_ki], w_in[:, e_ki:]
    wki2 = jnp.concatenate([wki, wki], axis=1)
    q, ktf, kt, vb, vtf, qi, kit2f, kit2, wi = norm_proj(
        xp, g, [(wq * LOG2E, False, [BF16]), (wk, True, [F32, BF16]), (wv, False, [BF16]), (wv, True, [F32]),
                (wqi, False, [BF16]), (wki2, True, [F32, BF16]), (wwi, False, [F32])], tm=512, bsz=bsz)
    o = dsa_prompt_attend(q, qi, wi, kit2, kt, vb, rel_bias, bsz, seq)
    yp = out_proj(o, w_out, xp, tm=512)
    k = jnp.transpose(ktf.reshape(bsz, DSA_HEADS, DSA_DH, seq), (0, 3, 1, 2))
    v = jnp.transpose(vtf.reshape(bsz, DSA_HEADS, DSA_DH, seq), (0, 3, 1, 2))
    ki = jnp.swapaxes(kit2f[:, :IDX_DIM, :], 1, 2)

    dbs = xs.shape[0]
    qs, ks, vs, qis, kis, wis = norm_proj(
        xs, g, [(wq, False, [BF16]), (wk, False, [F32]), (wv, False, [F32]), (wqi, False, [BF16]),
                (wki, False, [F32]), (wwi, False, [F32])], tm=dbs)
    past = page_table.shape[1] * PAGE_SIZE
    scores = dsa_sample_scores(page_table, qis, wis, kis, pool_ki)
    mask = dsa_sample_select(scores, min(DSA_TOPK, (past + 1) // 4))
    buckets = _t5_bucket_table(past)
    dist = np.maximum(past - np.arange(past + LANES), 0)
    bias = rel_bias[buckets[dist]].T.astype(F32)
    os_ = paged_attend("dsa", page_table, qs, ks, vs, pool_k, pool_v, (mask.reshape(dbs, 1, past + LANES), bias))
    ys = out_proj(os_, w_out, xs, tm=dbs)
    shp = lambda t, n: t.reshape(n, -1, DSA_HEADS, DSA_DH)
    return (yp, ys, (k, v, ki), (shp(ks, dbs), shp(vs, dbs), kis.reshape(dbs, 1, IDX_DIM)))


def _logf_rows_kernel(f_ref, b_ref, o_ref):
    o_ref[...] = _log_sigmoid(f_ref[...] + b_ref[...])


def fox_logf_rows(f, b_f):
    n, h = f.shape
    return pl.pallas_call(
        _logf_rows_kernel,
        grid=(1,),
        in_specs=[pl.BlockSpec((n, h), lambda i: (0, 0)), pl.BlockSpec((1, h), lambda i: (0, 0))],
        out_specs=pl.BlockSpec((n, h), lambda i: (0, 0)),
        out_shape=jax.ShapeDtypeStruct((n, h), F32),
        name="fox_logf_rows",
    )(f, b_f.reshape(1, h))


def kernel(x_prompt, x_sample, state_mlstm_C, state_mlstm_n, state_mlstm_m, cache_dsa_k, cache_dsa_v,
           cache_dsa_kidx, state_rglru_h, state_rglru_conv, cache_fox_k, cache_fox_v, cache_fox_logf, page_table,
           norm_mix, norm_mlp, norm_final, mlstm_w_in, mlstm_b_gate, mlstm_norm, mlstm_w_out, dsa_w_in, dsa_w_out,
           rel_bias, rglru_w_in, rglru_conv_w, rglru_conv_b, rglru_w_a, rglru_b_a, rglru_w_x, rglru_b_x, rglru_lam,
           rglru_w_out, fox_w_in, fox_b_f, fox_w_out, mlp_w1, mlp_w2):
    bsz, seq, d = x_prompt.shape
    dbs = x_sample.shape[0]
    assert x_sample.shape[1] == 1
    depth = norm_mix.shape[0]
    yp = x_prompt.reshape(bsz * seq, d)
    ys = x_sample.reshape(dbs, d)
    states = {kind: ([], []) for kind in range(4)}
    for layer in range(depth):
        kind, inst = layer % 4, layer // 4
        g = norm_mix[layer]
        if kind == 0:
            yp, ys, st_p, st_s = mlstm_layer(yp, ys, g, mlstm_w_in[inst], mlstm_b_gate[inst], mlstm_norm[inst],
                                             mlstm_w_out[inst], state_mlstm_C[inst], state_mlstm_n[inst],
                                             state_mlstm_m[inst], bsz, seq)
        elif kind == 1:
            yp, ys, st_p, st_s = dsa_layer(yp, ys, g, dsa_w_in[inst], dsa_w_out[inst], rel_bias, cache_dsa_k[inst],
                                           cache_dsa_v[inst], cache_dsa_kidx[inst], page_table, bsz, seq)
        elif kind == 2:
            yp, ys, st_p, st_s = rglru_layer(yp, ys, g, rglru_w_in[inst], rglru_conv_w[inst], rglru_conv_b[inst],
                                             rglru_w_a[inst], rglru_b_a[inst], rglru_w_x[inst], rglru_b_x[inst],
                                             rglru_lam[inst], rglru_w_out[inst], state_rglru_h[inst],
                                             state_rglru_conv[inst], bsz, seq)
        else:
            yp, ys, st_p, st_s = fox_layer(yp, ys, g, fox_w_in[inst], fox_b_f[inst], fox_w_out[inst],
                                           cache_fox_k[inst], cache_fox_v[inst], cache_fox_logf[inst], page_table,
                                           bsz, seq)
        states[kind][0].append(st_p)
        states[kind][1].append(st_s)
        last = layer == depth - 1
        yp = mlp(yp, norm_mlp[layer], mlp_w1[layer], mlp_w2[layer], norm_final, last, tm=1024, tf=512)
        ys = mlp(ys, norm_mlp[layer], mlp_w1[layer], mlp_w2[layer], norm_final, last, tm=dbs, tf=512)

    def stk(kind, group, j):
        return jnp.stack([entry[j] for entry in states[kind][group]])

    outs = [yp.reshape(bsz, seq, d), ys.reshape(dbs, 1, d)]
    for kind, n_leaves in ((0, 3), (1, 3), (2, 2), (3, 3)):
        for group in (0, 1):
            outs += [stk(kind, group, j) for j in range(n_leaves)]
    return tuple(outs)
```
